```python
import math
import jax, jax.numpy as jnp
from jax import lax
import numpy as np

D_MODEL = 2048
BATCH = 1
SEQ = 8192
DEPTH = 1
DEC_BATCH = 128
DEC_SEQ = 4
PAST_LEN = 16384
PAGE_SIZE = 128

QK_NOPE = 128
QK_ROPE = 64
V_HEAD = 128
N_HEADS = D_MODEL // V_HEAD
Q_LORA = 512
KV_LORA = 512
ROPE_THETA = 10000.0
SM_SCALE = 1.0 / math.sqrt(QK_NOPE + QK_ROPE)
Q_BLOCK = 128
C_CONV = D_MODEL // 2
CONV_WIDTH = 31
D_FF = -(-8 * D_MODEL // (3 * 256)) * 256
PLE_DIM = 256
EPS = 1e-6
OFF_GLU = 2 * C_CONV
OFF_Q = OFF_GLU + Q_LORA
OFF_KV = OFF_Q + KV_LORA
OFF_KR = OFF_KV + QK_ROPE
N_IN = OFF_KR + 2 * D_MODEL

kernel_name = "gated_conformer_conv_mla_decoder_step"


def rmsnorm(x, g):
    xf = x.astype(jnp.float32)
    y = xf * lax.rsqrt(jnp.mean(xf * xf, axis=-1, keepdims=True) + EPS)
    return (y * g.astype(jnp.float32)).astype(x.dtype)


def layernorm(x, g, b):
    xf = x.astype(jnp.float32)
    mu = jnp.mean(xf, axis=-1, keepdims=True)
    var = jnp.mean(jnp.square(xf - mu), axis=-1, keepdims=True)
    y = (xf - mu) * lax.rsqrt(var + EPS)
    return (y * g.astype(jnp.float32) + b.astype(jnp.float32)).astype(x.dtype)


def rope(x, pos):
    half = QK_ROPE // 2
    inv = ROPE_THETA ** (-jnp.arange(half, dtype=jnp.float32) / half)
    ang = pos[:, None] * inv[None, :]
    cos = jnp.cos(ang)[None, :, None, :]
    sin = jnp.sin(ang)[None, :, None, :]
    xf = x.astype(jnp.float32)
    x1, x2 = xf[..., :half], xf[..., half:]
    return jnp.concatenate([x1 * cos - x2 * sin, x2 * cos + x1 * sin], axis=-1).astype(x.dtype)


def mixer_inputs(h, pos, lw):
    proj = h @ lw["w_in"]
    a, b = proj[..., :C_CONV], proj[..., C_CONV:OFF_GLU]
    u = a * jax.nn.sigmoid(b)
    q_lat = rmsnorm(proj[..., OFF_GLU:OFF_Q], lw["g_q"])
    q = jnp.einsum('btl,lhd->bthd', q_lat, lw["w_uq"])
    q_nope, q_rope = q[..., :QK_NOPE], rope(q[..., QK_NOPE:], pos)
    c_kv = rmsnorm(proj[..., OFF_Q:OFF_KV], lw["g_kv"])
    k_rope = rope(proj[..., None, OFF_KV:OFF_KR], pos)[:, :, 0, :]
    gates = jax.nn.sigmoid(proj[..., OFF_KR:] + lw["b_gate"])
    return u, q_nope, q_rope, c_kv, k_rope, gates[..., :D_MODEL], gates[..., D_MODEL:]


def conv_branch(u_ext, lw):
    y = lax.conv_general_dilated(u_ext, lw["w_dw"][:, None, :], window_strides=(1,), padding='VALID',
                                 dimension_numbers=('NWC', 'WIO', 'NWC'), feature_group_count=C_CONV)
    y = jax.nn.silu(layernorm(y + lw["b_dw"], lw["g_conv_ln"], lw["b_conv_ln"]))
    return y @ lw["w_conv_out"]


def prompt_attention(q_nope, q_rope, c_kv, k_rope, w_ukv):
    B, S = q_nope.shape[0], q_nope.shape[1]
    k_nope = jnp.einsum('bsc,chn->bshn', c_kv, w_ukv[..., :QK_NOPE])
    v = jnp.einsum('bsc,chv->bshv', c_kv, w_ukv[..., QK_NOPE:])
    nb = S // Q_BLOCK
    qn_b = q_nope.reshape(B, nb, Q_BLOCK, N_HEADS, QK_NOPE).transpose(1, 0, 2, 3, 4)
    qr_b = q_rope.reshape(B, nb, Q_BLOCK, N_HEADS, QK_ROPE).transpose(1, 0, 2, 3, 4)
    key_pos = jnp.arange(S)

    def one_block(args):
        qn, qr, i = args
        s = (jnp.einsum('bqhn,bkhn->bhqk', qn, k_nope, preferred_element_type=jnp.float32)
             + jnp.einsum('bqhr,bkr->bhqk', qr, k_rope, preferred_element_type=jnp.float32)) * SM_SCALE
        q_pos = i * Q_BLOCK + jnp.arange(Q_BLOCK)
        s = jnp.where((key_pos[None, :] <= q_pos[:, None])[None, None], s, -jnp.inf)
        p = jax.nn.softmax(s, axis=-1).astype(v.dtype)
        return jnp.einsum('bhqk,bkhv->bqhv', p, v)

    o = lax.map(one_block, (qn_b, qr_b, jnp.arange(nb)))
    return o.transpose(1, 0, 2, 3, 4).reshape(B, S, N_HEADS, V_HEAD)


def sample_attention(q_nope, q_rope, c_kv, k_rope, ckv_pool, kr_pool, page_table, w_ukv):
    T = q_nope.shape[1]
    n_past = page_table.shape[1] * PAGE_SIZE
    q_lat = jnp.einsum('bthn,chn->bthc', q_nope, w_ukv[..., :QK_NOPE])
    tri = jnp.arange(T)[None, :] <= jnp.arange(T)[:, None]
    mask = jnp.concatenate([jnp.ones((T, n_past), dtype=bool), tri], axis=1)

    def one_seq(args):
        ql, qr, ck, kr, pages = args
        keys_c = jnp.concatenate([ckv_pool[pages].reshape(n_past, KV_LORA), ck], axis=0)
        keys_r = jnp.concatenate([kr_pool[pages].reshape(n_past, QK_ROPE), kr], axis=0)
        s = (jnp.einsum('thc,kc->htk', ql, keys_c, preferred_element_type=jnp.float32)
             + jnp.einsum('thr,kr->htk', qr, keys_r, preferred_element_type=jnp.float32)) * SM_SCALE
        s = jnp.where(mask[None], s, -jnp.inf)
        p = jax.nn.softmax(s, axis=-1).astype(keys_c.dtype)
        return jnp.einsum('htk,kc->thc', p, keys_c)

    o_lat = lax.map(one_seq, (q_lat, q_rope, c_kv, k_rope, page_table))
    return jnp.einsum('bthc,chv->bthv', o_lat, w_ukv[..., QK_NOPE:])


def trunk_layer(x, pemb, pos, conv_prev, attend, lw):
    B, T = x.shape[0], x.shape[1]
    h = rmsnorm(x, lw["g_mix_norm"])
    u, q_nope, q_rope, c_kv, k_rope, g_conv, g_attn = mixer_inputs(h, pos, lw)
    u_ext = jnp.concatenate([conv_prev.astype(u.dtype), u], axis=1)
    y_conv = conv_branch(u_ext, lw)
    y_attn = attend(q_nope, q_rope, c_kv, k_rope).reshape(B, T, N_HEADS * V_HEAD)
    x = x + (g_conv * y_conv + g_attn * y_attn) @ lw["w_out"]
    h2 = rmsnorm(x, lw["g_ffn_norm"])
    x = x + (jax.nn.silu(h2 @ lw["w_ffn_gate"]) * (h2 @ lw["w_ffn_up"])) @ lw["w_ffn_down"]
    gate = jax.nn.sigmoid(rmsnorm(x, lw["g_ple_norm"]) @ lw["w_ple_gate"])
    x = x + gate * (pemb @ lw["w_ple"])
    return x, c_kv, k_rope, u_ext[:, -(CONV_WIDTH - 1):]


def setup_inputs(seed: int = 0) -> dict:
    key = jax.random.key(seed)
    ks = jax.random.split(key, 40)
    f32 = jnp.float32
    n_pages = PAST_LEN // PAGE_SIZE
    n_used = DEC_BATCH * n_pages
    n_pool = n_used + n_used // 4

    def nrm(k, shape, scale):
        return jax.random.normal(k, shape, f32) * scale

    def gain(k, shape):
        return 1.0 + 0.05 * jax.random.normal(k, shape, f32)

    page_table = jax.random.permutation(ks[0], n_pool)[:n_used].reshape(DEC_BATCH, n_pages).astype(jnp.int32)
    return {
        "x_prompt": nrm(ks[1], (BATCH, SEQ, D_MODEL), 1.0),
        "x_sample": nrm(ks[2], (DEC_BATCH, DEC_SEQ, D_MODEL), 1.0),
        "p_prompt": nrm(ks[3], (DEPTH, BATCH, SEQ, PLE_DIM), 1.0),
        "p_sample": nrm(ks[4], (DEPTH, DEC_BATCH, DEC_SEQ, PLE_DIM), 1.0),
        "cache_ckv": nrm(ks[5], (DEPTH, n_pool, PAGE_SIZE, KV_LORA), 1.0),
        "cache_krope": nrm(ks[6], (DEPTH, n_pool, PAGE_SIZE, QK_ROPE), 1.0),
        "page_table": page_table,
        "state_conv": nrm(ks[7], (DEPTH, DEC_BATCH, CONV_WIDTH - 1, C_CONV), 0.5),
        "g_mix_norm": gain(ks[8], (DEPTH, D_MODEL)),
        "w_in": nrm(ks[9], (DEPTH, D_MODEL, N_IN), D_MODEL ** -0.5),
        "b_gate": nrm(ks[10], (DEPTH, 2 * D_MODEL), 0.02),
        "g_q": gain(ks[11], (DEPTH, Q_LORA)),
        "w_uq": nrm(ks[12], (DEPTH, Q_LORA, N_HEADS, QK_NOPE + QK_ROPE), Q_LORA ** -0.5),
        "g_kv": gain(ks[13], (DEPTH, KV_LORA)),
        "w_ukv": nrm(ks[14], (DEPTH, KV_LORA, N_HEADS, QK_NOPE + V_HEAD), KV_LORA ** -0.5),
        "w_dw": nrm(ks[15], (DEPTH, CONV_WIDTH, C_CONV), CONV_WIDTH ** -0.5),
        "b_dw": nrm(ks[16], (DEPTH, C_CONV), 0.02),
        "g_conv_ln": gain(ks[17], (DEPTH, C_CONV)),
        "b_conv_ln": nrm(ks[18], (DEPTH, C_CONV), 0.02),
        "w_conv_out": nrm(ks[19], (DEPTH, C_CONV, D_MODEL), C_CONV ** -0.5),
        "w_out": nrm(ks[20], (DEPTH, D_MODEL, D_MODEL), D_MODEL ** -0.5),
        "g_ffn_norm": gain(ks[21], (DEPTH, D_MODEL)),
        "w_ffn_gate": nrm(ks[22], (DEPTH, D_MODEL, D_FF), D_MODEL ** -0.5),
        "w_ffn_up": nrm(ks[23], (DEPTH, D_MODEL, D_FF), D_MODEL ** -0.5),
        "w_ffn_down": nrm(ks[24], (DEPTH, D_FF, D_MODEL), D_FF ** -0.5),
        "w_ple": nrm(ks[25], (DEPTH, PLE_DIM, D_MODEL), PLE_DIM ** -0.5),
        "g_ple_norm": gain(ks[26], (DEPTH, D_MODEL)),
        "w_ple_gate": nrm(ks[27], (DEPTH, D_MODEL, D_MODEL), D_MODEL ** -0.5),
        "g_final": gain(ks[28], (D_MODEL,)),
    }


def reference(x_prompt, x_sample, p_prompt, p_sample, cache_ckv, cache_krope, page_table, state_conv,
              g_mix_norm, w_in, b_gate, g_q, w_uq, g_kv, w_ukv, w_dw, b_dw, g_conv_ln, b_conv_ln,
              w_conv_out, w_out, g_ffn_norm, w_ffn_gate, w_ffn_up, w_ffn_down, w_ple, g_ple_norm,
              w_ple_gate, g_final):
    n_past = page_table.shape[1] * PAGE_SIZE
    pos_prompt = jnp.arange(x_prompt.shape[1], dtype=jnp.float32)
    pos_sample = n_past + jnp.arange(x_sample.shape[1], dtype=jnp.float32)
    xp, xs = x_prompt, x_sample
    ckv_p, kr_p, conv_p, ckv_s, kr_s, conv_s = [], [], [], [], [], []
    for i in range(DEPTH):
        lw = dict(g_mix_norm=g_mix_norm[i], w_in=w_in[i], b_gate=b_gate[i], g_q=g_q[i], w_uq=w_uq[i],
                  g_kv=g_kv[i], w_dw=w_dw[i], b_dw=b_dw[i], g_conv_ln=g_conv_ln[i], b_conv_ln=b_conv_ln[i],
                  w_conv_out=w_conv_out[i], w_out=w_out[i], g_ffn_norm=g_ffn_norm[i],
                  w_ffn_gate=w_ffn_gate[i], w_ffn_up=w_ffn_up[i], w_ffn_down=w_ffn_down[i],
                  w_ple=w_ple[i], g_ple_norm=g_ple_norm[i], w_ple_gate=w_ple_gate[i])
        w_ukv_i = w_ukv[i]
        ckv_pool_i, kr_pool_i = cache_ckv[i], cache_krope[i]

        def attend_prompt(qn, qr, ck, kr, w_ukv_i=w_ukv_i):
            return prompt_attention(qn, qr, ck, kr, w_ukv_i)

        def attend_sample(qn, qr, ck, kr, w_ukv_i=w_ukv_i, ckv_pool_i=ckv_pool_i, kr_pool_i=kr_pool_i):
            return sample_attention(qn, qr, ck, kr, ckv_pool_i, kr_pool_i, page_table, w_ukv_i)

        conv_zero = jnp.zeros((xp.shape[0], CONV_WIDTH - 1, C_CONV), dtype=xp.dtype)
        xp, ck, kr, cs = trunk_layer(xp, p_prompt[i], pos_prompt, conv_zero, attend_prompt, lw)
        ckv_p.append(ck); kr_p.append(kr); conv_p.append(cs)
        xs, ck, kr, cs = trunk_layer(xs, p_sample[i], pos_sample, state_conv[i], attend_sample, lw)
        ckv_s.append(ck); kr_s.append(kr); conv_s.append(cs)
    y_prompt = rmsnorm(xp, g_final)
    y_sample = rmsnorm(xs, g_final)
    return (y_prompt, y_sample, jnp.stack(ckv_p), jnp.stack(kr_p), jnp.stack(conv_p),
            jnp.stack(ckv_s), jnp.stack(kr_s), jnp.stack(conv_s))
```

```python
import functools
import math

import jax
import jax.numpy as jnp
from jax import lax
from jax.experimental import pallas as pl
from jax.experimental.pallas import tpu as pltpu

F32 = jnp.float32
BF16 = jnp.bfloat16

EPS = 1e-6
ROPE_THETA = 10000.0
PAGE_SIZE = 128
LANES = 128
MIB = 1024 * 1024

TM = 512
TM_OUT = 256
TM_CONV = 256
CONV_ROWS = 64
CONV_HALO = 32
SEQ_PER_STEP = 8
TQ = 512
PAGES_PER_STEP = 16
TF = 512


def _params(semantics, vmem_mib):
    return pltpu.CompilerParams(dimension_semantics=semantics, vmem_limit_bytes=vmem_mib * MIB)


def _dot(a, b):
    return jnp.dot(a, b, preferred_element_type=F32)


def _dot_t(a, b):
    return lax.dot_general(a, b, (((1,), (1,)), ((), ())), preferred_element_type=F32)


def _rms(x, g):
    return x * lax.rsqrt(jnp.mean(x * x, axis=-1, keepdims=True) + EPS) * g


def _row_spec(tm, n):
    return pl.BlockSpec((tm, n), lambda i: (i, 0))


def _full_spec(shape):
    return pl.BlockSpec(shape, lambda *_: (0,) * len(shape))


def _norm_kernel(x_ref, g_ref, o_ref):
    o_ref[...] = _rms(x_ref[...], g_ref[...]).astype(o_ref.dtype)


def _norm(x, g):
    t, d = x.shape
    return pl.pallas_call(
        _norm_kernel,
        grid=(t // TM,),
        in_specs=[_row_spec(TM, d), _full_spec((1, d))],
        out_specs=_row_spec(TM, d),
        out_shape=jax.ShapeDtypeStruct((t, d), BF16),
        compiler_params=_params(("parallel",), 32),
        name="mix_norm",
    )(x, g)


def _glu_kernel(h_ref, wa_ref, wb_ref, o_ref):
    h = h_ref[...]
    o_ref[...] = _dot(h, wa_ref[...]) * jax.nn.sigmoid(_dot(h, wb_ref[...]))


def _glu(h, wa, wb):
    t, d = h.shape
    c = wa.shape[1]
    return pl.pallas_call(
        _glu_kernel,
        grid=(t // TM,),
        in_specs=[_row_spec(TM, d), _full_spec((d, c)), _full_spec((d, c))],
        out_specs=_row_spec(TM, c),
        out_shape=jax.ShapeDtypeStruct((t, c), F32),
        compiler_params=_params(("parallel",), 40),
        name="glu",
    )(h, wa, wb)


def _q_kernel(h_ref, wq_ref, gq_ref, wn_ref, wr_ref, wt_ref, cs_ref, sn_ref, o_ref, *, n_heads, scale):
    ql = _rms(_dot(h_ref[...], wq_ref[...]), gq_ref[...]).astype(BF16)
    qn = _dot(ql, wn_ref[...]) * scale
    qr = _dot(ql, wr_ref[...])
    qt = _dot(ql, wt_ref[...])
    cs = cs_ref[...]
    sn = sn_ref[...]
    for h in range(n_heads):
        lo = h * LANES
        o_ref[:, 2 * lo:2 * lo + LANES] = qn[:, lo:lo + LANES].astype(o_ref.dtype)
        rot = qr[:, lo:lo + LANES] * cs + qt[:, lo:lo + LANES] * sn
        o_ref[:, 2 * lo + LANES:2 * lo + 2 * LANES] = rot.astype(o_ref.dtype)


def _q_proj(h, wq, gq, wn, wr, wt, cs, sn, n_heads, scale):
    t, d = h.shape
    ql = wq.shape[1]
    hw = wn.shape[1]
    return pl.pallas_call(
        functools.partial(_q_kernel, n_heads=n_heads, scale=scale),
        grid=(t // TM,),
        in_specs=[_row_spec(TM, d), _full_spec((d, ql)), _full_spec((1, ql)), _full_spec((ql, hw)),
                  _full_spec((ql, hw)), _full_spec((ql, hw)), _row_spec(TM, LANES), _row_spec(TM, LANES)],
        out_specs=_row_spec(TM, 2 * hw),
        out_shape=jax.ShapeDtypeStruct((t, 2 * hw), BF16),
        compiler_params=_params(("parallel",), 48),
        name="q_proj",
    )(h, wq, gq, wn, wr, wt, cs, sn)


def _kv_kernel(h_ref, wkv_ref, gkv_ref, wkr_ref, wkt_ref, wuk_ref, wuv_ref, cs_ref, sn_ref,
               ckv_ref, kr_ref, kcat_ref, v_ref, *, n_heads, rope_dim):
    h = h_ref[...]
    ckv = _rms(_dot(h, wkv_ref[...]), gkv_ref[...])
    ckv_ref[...] = ckv
    kr = _dot(h, wkr_ref[...]) * cs_ref[...] + _dot(h, wkt_ref[...]) * sn_ref[...]
    kr_ref[...] = kr[:, :rope_dim]
    kr_b = kr.astype(kcat_ref.dtype)
    c = ckv.astype(BF16)
    kn = _dot(c, wuk_ref[...]).astype(kcat_ref.dtype)
    v_ref[...] = _dot(c, wuv_ref[...]).astype(v_ref.dtype)
    for hd in range(n_heads):
        lo = hd * LANES
        kcat_ref[:, 2 * lo:2 * lo + LANES] = kn[:, lo:lo + LANES]
        kcat_ref[:, 2 * lo + LANES:2 * lo + 2 * LANES] = kr_b


def _kv_proj(h, wkv, gkv, wkr, wkt, wuk, wuv, cs, sn, n_heads, rope_dim):
    t, d = h.shape
    kl = wkv.shape[1]
    hw = wuk.shape[1]
    return pl.pallas_call(
        functools.partial(_kv_kernel, n_heads=n_heads, rope_dim=rope_dim),
        grid=(t // TM,),
        in_specs=[_row_spec(TM, d), _full_spec((d, kl)), _full_spec((1, kl)), _full_spec((d, LANES)),
                  _full_spec((d, LANES)), _full_spec((kl, hw)), _full_spec((kl, hw)),
                  _row_spec(TM, LANES), _row_spec(TM, LANES)],
        out_specs=[_row_spec(TM, kl), _row_spec(TM, rope_dim), _row_spec(TM, 2 * hw), _row_spec(TM, hw)],
        out_shape=[jax.ShapeDtypeStruct((t, kl), F32), jax.ShapeDtypeStruct((t, rope_dim), F32),
                   jax.ShapeDtypeStruct((t, 2 * hw), BF16), jax.ShapeDtypeStruct((t, hw), BF16)],
        compiler_params=_params(("parallel",), 48),
        name="kv_proj",
    )(h, wkv, gkv, wkr, wkt, wuk, wuv, cs, sn)


def _gate_kernel(h_ref, w_ref, b_ref, o_ref):
    o_ref[...] = jax.nn.sigmoid(_dot(h_ref[...], w_ref[...]) + b_ref[...]).astype(o_ref.dtype)


def _gates(h, w, b, tn=1024):
    t, d = h.shape
    n = w.shape[1]
    return pl.pallas_call(
        _gate_kernel,
        grid=(t // TM, n // tn),
        in_specs=[pl.BlockSpec((TM, d), lambda i, j: (i, 0)), pl.BlockSpec((d, tn), lambda i, j: (0, j)),
                  pl.BlockSpec((1, tn), lambda i, j: (0, j))],
        out_specs=pl.BlockSpec((TM, tn), lambda i, j: (i, j)),
        out_shape=jax.ShapeDtypeStruct((t, n), BF16),
        compiler_params=_params(("parallel", "arbitrary"), 40),
        name="branch_gates",
    )(h, w, b)


def _ln_silu(y, g, b):
    mu = jnp.mean(y, axis=-1, keepdims=True)
    d = y - mu
    var = jnp.mean(d * d, axis=-1, keepdims=True)
    z = d * lax.rsqrt(var + EPS) * g + b
    return z * jax.nn.sigmoid(z)


def _conv_prompt_kernel(prev_ref, cur_ref, w_ref, bdw_ref, g_ref, b_ref, o_ref, ext_ref, y_ref, *, width):
    i = pl.program_id(0)
    tm, ch = cur_ref.shape

    @pl.when(i == 0)
    def _():
        ext_ref[0:CONV_HALO, :] = jnp.zeros((CONV_HALO, ch), F32)

    @pl.when(i > 0)
    def _():
        ext_ref[0:CONV_HALO, :] = prev_ref[...]

    ext_ref[CONV_HALO:CONV_HALO + tm, :] = cur_ref[...]

    lead = CONV_HALO - (width - 1)
    by_shift = {}
    for k in range(width):
        a, s = divmod(lead + k, 8)
        by_shift.setdefault(s, []).append((a, k))
    for r0 in range(0, tm, CONV_ROWS):
        for c0 in range(0, ch, LANES):
            acc = jnp.zeros((CONV_ROWS, LANES), F32) + bdw_ref[:, c0:c0 + LANES]
            for s, taps in by_shift.items():
                a_lo = min(a for a, _ in taps)
                a_hi = max(a for a, _ in taps)
                win = ext_ref[r0 + s + 8 * a_lo:r0 + s + 8 * a_hi + CONV_ROWS, c0:c0 + LANES]
                for a, k in taps:
                    off = 8 * (a - a_lo)
                    acc = acc + win[off:off + CONV_ROWS, :] * w_ref[k:k + 1, c0:c0 + LANES]
            y_ref[r0:r0 + CONV_ROWS, c0:c0 + LANES] = acc
    o_ref[...] = _ln_silu(y_ref[...], g_ref[...], b_ref[...]).astype(o_ref.dtype)


def _conv_prompt(u_all, t_prompt, w_dw, b_dw, g_ln, b_ln):
    t_all, ch = u_all.shape
    width = w_dw.shape[0]
    per = TM_CONV // CONV_HALO
    return pl.pallas_call(
        functools.partial(_conv_prompt_kernel, width=width),
        grid=(t_prompt // TM_CONV,),
        in_specs=[pl.BlockSpec((CONV_HALO, ch), lambda i: (jnp.maximum(i * per - 1, 0), 0)),
                  _row_spec(TM_CONV, ch), _full_spec((width, ch)), _full_spec((1, ch)),
                  _full_spec((1, ch)), _full_spec((1, ch))],
        out_specs=_row_spec(TM_CONV, ch),
        out_shape=jax.ShapeDtypeStruct((t_all, ch), BF16),
        scratch_shapes=[pltpu.VMEM((TM_CONV + CONV_HALO, ch), F32), pltpu.VMEM((TM_CONV, ch), F32)],
        compiler_params=_params(("arbitrary",), 32),
        name="conv_prompt",
    )(u_all, u_all, w_dw, b_dw, g_ln, b_ln)


def _conv_sample_kernel(z_hbm_ref, st_ref, u_ref, w_ref, bdw_ref, g_ref, b_ref, o_ref, ext_ref, y_ref):
    del z_hbm_ref
    nseq, n_state, ch = st_ref.shape
    n_new = u_ref.shape[1]
    base = ext_ref.shape[0] - 8
    for s in range(nseq):
        ext_ref[base - 8:base + 8, :] = jnp.zeros((16, ch), F32)
        ext_ref[0:n_state, :] = st_ref[s]
        ext_ref[base:base + n_new, :] = u_ref[s]
        ext = ext_ref[...]
        for t in range(n_new):
            row = jnp.sum(ext * w_ref[t], axis=0, keepdims=True) + bdw_ref[...]
            y_ref[s * n_new + t:s * n_new + t + 1, :] = row
    o_ref[...] = _ln_silu(y_ref[...], g_ref[...], b_ref[...]).astype(o_ref.dtype)


def _sample_conv_weights(w_dw, n_state, n_new):
    width, ch = w_dw.shape
    assert width == n_state + 1 and n_new <= 8
    base = -(-n_state // 8) * 8
    out = []
    for t in range(n_new):
        wt = jnp.zeros((base + 8, ch), F32)
        wt = wt.at[t:n_state].set(w_dw[0:n_state - t])
        wt = wt.at[base:base + t + 1].set(w_dw[n_state - t:n_state + 1])
        out.append(wt)
    return jnp.stack(out)


def _conv_sample(z_all, u_all, t_prompt, state, w_dw, b_dw, g_ln, b_ln):
    t_all, ch = u_all.shape
    nb, n_state, _ = state.shape
    n_new = (t_all - t_prompt) // nb
    rows = SEQ_PER_STEP * n_new
    u3 = u_all.reshape(t_all // n_new, n_new, ch)
    w_s = _sample_conv_weights(w_dw, n_state, n_new)
    ext_rows = w_s.shape[1]
    return pl.pallas_call(
        _conv_sample_kernel,
        grid=(nb // SEQ_PER_STEP,),
        in_specs=[pl.BlockSpec(memory_space=pl.ANY),
                  pl.BlockSpec((SEQ_PER_STEP, n_state, ch), lambda i: (i, 0, 0)),
                  pl.BlockSpec((SEQ_PER_STEP, n_new, ch), lambda i: (t_prompt // rows + i, 0, 0)),
                  _full_spec((n_new, ext_rows, ch)), _full_spec((1, ch)), _full_spec((1, ch)), _full_spec((1, ch))],
        out_specs=pl.BlockSpec((rows, ch), lambda i: (t_prompt // rows + i, 0)),
        out_shape=jax.ShapeDtypeStruct(z_all.shape, z_all.dtype),
        scratch_shapes=[pltpu.VMEM((ext_rows, ch), F32), pltpu.VMEM((rows, ch), F32)],
        input_output_aliases={0: 0},
        compiler_params=_params(("arbitrary",), 32),
        name="conv_sample",
    )(z_all, state, u3, w_s, b_dw, g_ln, b_ln)


def _softmax_step(s, v, m, l, acc):
    m_new = jnp.maximum(m, jnp.max(s, axis=-1, keepdims=True))
    alpha = jnp.exp(m - m_new)
    p = jnp.exp(s - m_new)
    l_new = alpha * l + jnp.sum(p, axis=-1, keepdims=True)
    acc_new = alpha * acc + _dot(p.astype(v.dtype), v)
    return m_new, l_new, acc_new


def _flash_kernel(q_ref, k_ref, v_ref, o_ref):
    qi = pl.program_id(1)
    tq = q_ref.shape[0]
    dv = v_ref.shape[1]
    q = q_ref[...]

    def body(kb, carry):
        off = pl.multiple_of(kb * tq, tq)
        s = _dot_t(q, k_ref[pl.ds(off, tq), :])
        return _softmax_step(s, v_ref[pl.ds(off, tq), :], *carry)

    init = (jnp.full((tq, 1), -jnp.inf, F32), jnp.zeros((tq, 1), F32), jnp.zeros((tq, dv), F32))
    carry = lax.fori_loop(0, qi, body, init)

    off = pl.multiple_of(qi * tq, tq)
    s = _dot_t(q, k_ref[pl.ds(off, tq), :])
    visible = lax.broadcasted_iota(jnp.int32, (tq, tq), 1) <= lax.broadcasted_iota(jnp.int32, (tq, tq), 0)
    s = jnp.where(visible, s, -jnp.inf)
    _, l, acc = _softmax_step(s, v_ref[pl.ds(off, tq), :], *carry)
    o_ref[...] = (acc / l).astype(o_ref.dtype)


def _flash(qcat, kcat, v, t_prompt, n_heads):
    t_all = qcat.shape[0]
    dq = qcat.shape[1] // n_heads
    dv = v.shape[1] // n_heads
    return pl.pallas_call(
        _flash_kernel,
        grid=(n_heads, t_prompt // TQ),
        in_specs=[pl.BlockSpec((TQ, dq), lambda h, i: (i, h)),
                  pl.BlockSpec((t_prompt, dq), lambda h, i: (0, h)),
                  pl.BlockSpec((t_prompt, dv), lambda h, i: (0, h))],
        out_specs=pl.BlockSpec((TQ, dv), lambda h, i: (i, h)),
        out_shape=jax.ShapeDtypeStruct((t_all, n_heads * dv), BF16),
        compiler_params=_params(("parallel", "arbitrary"), 48),
        name="prompt_attention",
    )(qcat, kcat, v)


def _absorb_kernel(q_ref, w_ref, o_ref):
    o_ref[...] = _dot(q_ref[...], w_ref[...]).astype(o_ref.dtype)


def _absorb(qcat, t_prompt, wuk_t):
    n_heads, dn, kl = wuk_t.shape
    ts = qcat.shape[0] - t_prompt
    return pl.pallas_call(
        _absorb_kernel,
        grid=(n_heads,),
        in_specs=[pl.BlockSpec((ts, dn), lambda h: (t_prompt // ts, 2 * h)),
                  pl.BlockSpec((None, dn, kl), lambda h: (h, 0, 0))],
        out_specs=pl.BlockSpec((ts, kl), lambda h: (0, h)),
        out_shape=jax.ShapeDtypeStruct((ts, n_heads * kl), BF16),
        compiler_params=_params(("parallel",), 32),
        name="absorb_q",
    )(qcat, wuk_t)


def _sample_attn_kernel(pt_ref, qa_ref, qr_ref, cn_ref, kn_ref, *rest, n_heads, rope_dim):
    del pt_ref
    npg = PAGES_PER_STEP
    kc_refs, kr_refs = rest[:npg], rest[npg:2 * npg]
    o_ref, kc_s, kr_s, m_s, l_s, acc_s = rest[2 * npg:]
    c = pl.program_id(1)
    qa = qa_ref[...]
    qr = qr_ref[:, :rope_dim]
    rows = qa.shape[0]

    @pl.when(c == 0)
    def _():
        qa32 = qa.astype(F32)
        qr32 = qr.astype(F32)
        n_new = cn_ref.shape[0]
        tok = lax.broadcasted_iota(jnp.int32, (rows, 1), 0) // n_heads
        scores = []
        for j in range(n_new):
            sj = (jnp.sum(qa32 * cn_ref[j:j + 1, :], axis=-1, keepdims=True)
                  + jnp.sum(qr32 * kn_ref[j:j + 1, :], axis=-1, keepdims=True))
            scores.append(jnp.where(tok >= j, sj, -jnp.inf))
        m = scores[0]
        for sj in scores[1:]:
            m = jnp.maximum(m, sj)
        l = jnp.zeros_like(m)
        acc = jnp.zeros(acc_s.shape, F32)
        for j, sj in enumerate(scores):
            pj = jnp.exp(sj - m)
            l = l + pj
            acc = acc + pj * cn_ref[j:j + 1, :]
        m_s[...] = m
        l_s[...] = l
        acc_s[...] = acc

    for i in range(npg):
        kc_s[i * PAGE_SIZE:(i + 1) * PAGE_SIZE, :] = kc_refs[i][...].astype(kc_s.dtype)
        kr_s[i * PAGE_SIZE:(i + 1) * PAGE_SIZE, :] = kr_refs[i][...].astype(kr_s.dtype)
    kc = kc_s[...]
    s = _dot_t(qa, kc) + _dot_t(qr, kr_s[...])
    m, l, acc = _softmax_step(s, kc, m_s[...], l_s[...], acc_s[...])
    m_s[...] = m
    l_s[...] = l
    acc_s[...] = acc

    @pl.when(c == pl.num_programs(1) - 1)
    def _():
        o_ref[...] = (acc / l).astype(o_ref.dtype)


def _sample_attn(page_table, qabs, qcat, ckv_all, kr_all, pool_c, pool_r, t_prompt, n_heads):
    nb, n_pages = page_table.shape
    kl = pool_c.shape[-1]
    rope_dim = pool_r.shape[-1]
    ts = qabs.shape[0]
    n_new = ts // nb
    rows = n_new * n_heads
    npg = PAGES_PER_STEP
    dq = qcat.shape[1] // n_heads
    qa3 = qabs.reshape(nb, rows, kl)
    qc3 = qcat.reshape(qcat.shape[0] // n_new, rows, dq)
    cn3 = ckv_all.reshape(ckv_all.shape[0] // n_new, n_new, kl)
    kn3 = kr_all.reshape(kr_all.shape[0] // n_new, n_new, rope_dim)
    first = t_prompt // n_new
    pt = page_table.reshape(-1)

    def page_spec(width, i):
        return pl.BlockSpec((None, None, PAGE_SIZE, width),
                            lambda b, c, pt_ref: (0, pt_ref[b * n_pages + c * npg + i], 0, 0))

    in_specs = ([pl.BlockSpec((None, rows, kl), lambda b, c, pt_ref: (b, 0, 0)),
                 pl.BlockSpec((None, rows, LANES), lambda b, c, pt_ref: (first + b, 0, 1)),
                 pl.BlockSpec((None, n_new, kl), lambda b, c, pt_ref: (first + b, 0, 0)),
                 pl.BlockSpec((None, n_new, rope_dim), lambda b, c, pt_ref: (first + b, 0, 0))]
                + [page_spec(kl, i) for i in range(npg)]
                + [page_spec(rope_dim, i) for i in range(npg)])
    grid_spec = pltpu.PrefetchScalarGridSpec(
        num_scalar_prefetch=1,
        grid=(nb, n_pages // npg),
        in_specs=in_specs,
        out_specs=pl.BlockSpec((None, rows, kl), lambda b, c, pt_ref: (b, 0, 0)),
        scratch_shapes=[pltpu.VMEM((npg * PAGE_SIZE, kl), BF16), pltpu.VMEM((npg * PAGE_SIZE, rope_dim), BF16),
                        pltpu.VMEM((rows, 1), F32), pltpu.VMEM((rows, 1), F32), pltpu.VMEM((rows, kl), F32)],
    )
    return pl.pallas_call(
        functools.partial(_sample_attn_kernel, n_heads=n_heads, rope_dim=rope_dim),
        grid_spec=grid_spec,
        out_shape=jax.ShapeDtypeStruct((nb, rows, kl), BF16),
        compiler_params=_params(("parallel", "arbitrary"), 40),
        name="sample_attention",
    )(pt, qa3, qc3, cn3, kn3, *([pool_c] * npg), *([pool_r] * npg))


def _uv_kernel(y_hbm_ref, o_ref_in, w_ref, out_ref):
    del y_hbm_ref
    out_ref[...] = _dot(o_ref_in[...], w_ref[...]).astype(out_ref.dtype)


def _uv(y_attn, olat, wuv_h, t_prompt):
    n_heads, kl, dv = wuv_h.shape
    ts = olat.shape[0]
    return pl.pallas_call(
        _uv_kernel,
        grid=(n_heads,),
        in_specs=[pl.BlockSpec(memory_space=pl.ANY),
                  pl.BlockSpec((ts, kl), lambda h: (0, h)),
                  pl.BlockSpec((None, kl, dv), lambda h: (h, 0, 0))],
        out_specs=pl.BlockSpec((ts, dv), lambda h: (t_prompt // ts, h)),
        out_shape=jax.ShapeDtypeStruct(y_attn.shape, y_attn.dtype),
        input_output_aliases={0: 0},
        compiler_params=_params(("arbitrary",), 32),
        name="value_up",
    )(y_attn, olat, wuv_h)


def _out_kernel(x_ref, z_ref, ya_ref, g_ref, wc_ref, wo_ref, gn_ref, xo_ref, ho_ref):
    d = x_ref.shape[1]
    yc = _dot(z_ref[...], wc_ref[...])
    mix = g_ref[:, :d].astype(F32) * yc + g_ref[:, d:].astype(F32) * ya_ref[...].astype(F32)
    x1 = x_ref[...] + _dot(mix.astype(BF16), wo_ref[...])
    xo_ref[...] = x1
    ho_ref[...] = _rms(x1, gn_ref[...]).astype(ho_ref.dtype)


def _out_proj(x, z, y_attn, gates, w_conv_out, w_out, g_ffn):
    t, d = x.shape
    ch = z.shape[1]
    return pl.pallas_call(
        _out_kernel,
        grid=(t // TM_OUT,),
        in_specs=[_row_spec(TM_OUT, d), _row_spec(TM_OUT, ch), _row_spec(TM_OUT, d), _row_spec(TM_OUT, 2 * d),
                  _full_spec((ch, d)), _full_spec((d, d)), _full_spec((1, d))],
        out_specs=[_row_spec(TM_OUT, d), _row_spec(TM_OUT, d)],
        out_shape=[jax.ShapeDtypeStruct((t, d), F32), jax.ShapeDtypeStruct((t, d), BF16)],
        compiler_params=_params(("parallel",), 56),
        name="out_proj",
    )(x, z, y_attn, gates, w_conv_out, w_out, g_ffn)


def _ffn_kernel(h_ref, x_ref, wg_ref, wu_ref, wd_ref, gn_ref, xo_ref, ho_ref, acc_ref):
    f = pl.program_id(1)

    @pl.when(f == 0)
    def _():
        acc_ref[...] = x_ref[...]

    h = h_ref[...]
    g = _dot(h, wg_ref[...])
    a = (g * jax.nn.sigmoid(g) * _dot(h, wu_ref[...])).astype(BF16)
    acc_ref[...] += _dot(a, wd_ref[...])

    @pl.when(f == pl.num_programs(1) - 1)
    def _():
        x2 = acc_ref[...]
        xo_ref[...] = x2
        ho_ref[...] = _rms(x2, gn_ref[...]).astype(ho_ref.dtype)


def _ffn(h2, x1, wg, wu, wd, g_ple):
    t, d = x1.shape
    dff = wg.shape[1]
    return pl.pallas_call(
        _ffn_kernel,
        grid=(t // TM, dff // TF),
        in_specs=[pl.BlockSpec((TM, d), lambda i, f: (i, 0)), pl.BlockSpec((TM, d), lambda i, f: (i, 0)),
                  pl.BlockSpec((d, TF), lambda i, f: (0, f)), pl.BlockSpec((d, TF), lambda i, f: (0, f)),
                  pl.BlockSpec((TF, d), lambda i, f: (f, 0)), pl.BlockSpec((1, d), lambda i, f: (0, 0))],
        out_specs=[pl.BlockSpec((TM, d), lambda i, f: (i, 0)), pl.BlockSpec((TM, d), lambda i, f: (i, 0))],
        out_shape=[jax.ShapeDtypeStruct((t, d), F32), jax.ShapeDtypeStruct((t, d), BF16)],
        scratch_shapes=[pltpu.VMEM((TM, d), F32)],
        compiler_params=_params(("parallel", "arbitrary"), 56),
        name="swiglu",
    )(h2, x1, wg, wu, wd, g_ple)


def _ple_kernel(h_ref, x_ref, p_ref, wg_ref, wp_ref, gf_ref, y_ref):
    gate = jax.nn.sigmoid(_dot(h_ref[...], wg_ref[...]))
    x3 = x_ref[...] + gate * _dot(p_ref[...].astype(BF16), wp_ref[...])
    y_ref[...] = _rms(x3, gf_ref[...])


def _ple_final(h3, x2, pemb, w_gate, w_ple, g_final, first_tile):
    n, pd = pemb.shape
    d = x2.shape[1]
    return pl.pallas_call(
        _ple_kernel,
        grid=(n // TM,),
        in_specs=[pl.BlockSpec((TM, d), lambda i: (first_tile + i, 0)),
                  pl.BlockSpec((TM, d), lambda i: (first_tile + i, 0)),
                  _row_spec(TM, pd), _full_spec((d, d)), _full_spec((pd, d)), _full_spec((1, d))],
        out_specs=_row_spec(TM, d),
        out_shape=jax.ShapeDtypeStruct((n, d), F32),
        compiler_params=_params(("parallel",), 48),
        name="ple_final",
    )(h3, x2, pemb, w_gate, w_ple, g_final)


def _rope_tables(pos, half, scale):
    inv = ROPE_THETA ** (-jnp.arange(half, dtype=F32) / half)
    ang = pos[:, None] * inv[None, :]
    cos = jnp.cos(ang)
    sin = jnp.sin(ang)
    pad = jnp.zeros((pos.shape[0], LANES - 2 * half), F32)
    cs = jnp.concatenate([cos, cos, pad], axis=-1)
    sn = jnp.concatenate([sin, sin, pad], axis=-1)
    return cs, sn, cs * scale, sn * scale


def _rot_cols(w, half):
    return jnp.concatenate([-w[..., half:], w[..., :half]], axis=-1)


def _pad_lanes(w):
    return jnp.concatenate([w, jnp.zeros(w.shape[:-1] + (LANES - w.shape[-1],), w.dtype)], axis=-1)


def kernel(x_prompt, x_sample, p_prompt, p_sample, cache_ckv, cache_krope, page_table, state_conv, g_mix_norm, w_in, b_gate, g_q, w_uq, g_kv, w_ukv, w_dw, b_dw, g_conv_ln, b_conv_ln, w_conv_out, w_out, g_ffn_norm, w_ffn_gate, w_ffn_up, w_ffn_down, w_ple, g_ple_norm, w_ple_gate, g_final):
    depth = w_in.shape[0]
    assert depth == 1, "single-layer step only"
    batch, seq, d = x_prompt.shape
    nb, n_new, _ = x_sample.shape
    assert batch == 1
    t_prompt = batch * seq
    t_sample = nb * n_new
    ch = w_dw.shape[2]
    q_lora = g_q.shape[1]
    kv_lora = g_kv.shape[1]
    n_heads = w_uq.shape[2]
    rope_dim = cache_krope.shape[-1]
    nope_dim = w_uq.shape[3] - rope_dim
    half = rope_dim // 2
    scale = 1.0 / math.sqrt(nope_dim + rope_dim)
    assert nope_dim == LANES and w_ukv.shape[3] - nope_dim == LANES
    assert t_prompt % TM == 0 and t_sample == TM and t_prompt % TM_CONV == 0 and t_prompt % TQ == 0
    assert nb % SEQ_PER_STEP == 0 and page_table.shape[1] % PAGES_PER_STEP == 0

    off_glu = 2 * ch
    off_q = off_glu + q_lora
    off_kv = off_q + kv_lora
    off_kr = off_kv + rope_dim

    wi = w_in[0]
    wa = wi[:, :ch].astype(BF16)
    wb = wi[:, ch:off_glu].astype(BF16)
    wq = wi[:, off_glu:off_q].astype(BF16)
    wkv = wi[:, off_q:off_kv].astype(BF16)
    wkr = wi[:, off_kv:off_kr]
    wkr_p = _pad_lanes(wkr).astype(BF16)
    wkt_p = _pad_lanes(_rot_cols(wkr, half)).astype(BF16)
    wgt = wi[:, off_kr:].astype(BF16)
    wq_n = w_uq[0][:, :, :nope_dim].reshape(q_lora, n_heads * nope_dim).astype(BF16)
    wq_rope = w_uq[0][:, :, nope_dim:]
    wq_r = _pad_lanes(wq_rope).reshape(q_lora, n_heads * LANES).astype(BF16)
    wq_t = _pad_lanes(_rot_cols(wq_rope, half)).reshape(q_lora, n_heads * LANES).astype(BF16)
    wuk = w_ukv[0][:, :, :nope_dim]
    wuv = w_ukv[0][:, :, nope_dim:]
    wuk_flat = wuk.reshape(kv_lora, n_heads * nope_dim).astype(BF16)
    wuv_flat = wuv.reshape(kv_lora, n_heads * LANES).astype(BF16)
    wuk_t = jnp.transpose(wuk, (1, 2, 0)).astype(BF16)
    wuv_h = jnp.transpose(wuv, (1, 0, 2)).astype(BF16)
    row = lambda a: a.reshape(1, -1)

    n_past = page_table.shape[1] * PAGE_SIZE
    pos = jnp.concatenate([jnp.arange(seq, dtype=F32), jnp.tile(n_past + jnp.arange(n_new, dtype=F32), nb)])
    cs_k, sn_k, cs_q, sn_q = _rope_tables(pos, half, scale)

    x_all = jnp.concatenate([x_prompt.reshape(t_prompt, d), x_sample.reshape(t_sample, d)], axis=0)

    h = _norm(x_all, row(g_mix_norm[0]))
    u_all = _glu(h, wa, wb)
    qcat = _q_proj(h, wq, row(g_q[0]), wq_n, wq_r, wq_t, cs_q, sn_q, n_heads, scale)
    ckv_all, kr_all, kcat, v = _kv_proj(h, wkv, row(g_kv[0]), wkr_p, wkt_p, wuk_flat, wuv_flat, cs_k, sn_k,
                                        n_heads, rope_dim)
    gates = _gates(h, wgt, row(b_gate[0]))

    wdw = w_dw[0]
    z = _conv_prompt(u_all, t_prompt, wdw, row(b_dw[0]), row(g_conv_ln[0]), row(b_conv_ln[0]))
    z = _conv_sample(z, u_all, t_prompt, state_conv[0], wdw, row(b_dw[0]), row(g_conv_ln[0]), row(b_conv_ln[0]))

    y_attn = _flash(qcat, kcat, v, t_prompt, n_heads)
    qabs = _absorb(qcat, t_prompt, wuk_t)
    olat = _sample_attn(page_table, qabs, qcat, ckv_all, kr_all, cache_ckv, cache_krope, t_prompt, n_heads)
    y_attn = _uv(y_attn, olat.reshape(t_sample, n_heads * kv_lora), wuv_h, t_prompt)

    x1, h2 = _out_proj(x_all, z, y_attn, gates, w_conv_out[0].astype(BF16), w_out[0].astype(BF16),
                       row(g_ffn_norm[0]))
    x2, h3 = _ffn(h2, x1, w_ffn_gate[0].astype(BF16), w_ffn_up[0].astype(BF16), w_ffn_down[0].astype(BF16),
                  row(g_ple_norm[0]))
    wpg = w_ple_gate[0].astype(BF16)
    wp = w_ple[0].astype(BF16)
    gf = row(g_final)
    y_prompt = _ple_final(h3, x2, p_prompt[0].reshape(t_prompt, -1), wpg, wp, gf, 0)
    y_sample = _ple_final(h3, x2, p_sample[0].reshape(t_sample, -1), wpg, wp, gf, t_prompt // TM)

    u_s = u_all[t_prompt:].reshape(nb, n_new, ch)
    n_state = state_conv.shape[2]
    conv_p = u_all[t_prompt - n_state:t_prompt].reshape(1, batch, n_state, ch)
    conv_s = jnp.concatenate([state_conv[0][:, n_new:], u_s], axis=1)[None]
    return (y_prompt.reshape(batch, seq, d), y_sample.reshape(nb, n_new, d),
            ckv_all[:t_prompt].reshape(1, batch, seq, kv_lora), kr_all[:t_prompt].reshape(1, batch, seq, rope_dim),
            conv_p,
            ckv_all[t_prompt:].reshape(1, nb, n_new, kv_lora), kr_all[t_prompt:].reshape(1, nb, n_new, rope_dim),
            conv_s)
```

```python
import functools
import math

import jax
import jax.numpy as jnp
from jax import lax
from jax.experimental import pallas as pl
from jax.experimental.pallas import tpu as pltpu

F32 = jnp.float32
BF16 = jnp.bfloat16

EPS = 1e-6
ROPE_THETA = 10000.0
PAGE_SIZE = 128
LANES = 128
BF16_ROWS = 16
MIB = 1024 * 1024
LOG2E = math.log2(math.e)

TM = 512
TM_OUT = 256
TM_CONV = 256
CONV_ROWS = 64
CONV_HALO = 32
SEQ_PER_STEP = 32
TQ = 1024
TK = 512
FLASH_HEADS = 2
PAGES_PER_STEP = 32
PAGE_GROUPS = 1
TF = 512


def _params(semantics, vmem_mib):
    return pltpu.CompilerParams(dimension_semantics=semantics, vmem_limit_bytes=vmem_mib * MIB)


def _dot(a, b):
    return jnp.dot(a, b, preferred_element_type=F32)


def _dot_t(a, b):
    return lax.dot_general(a, b, (((1,), (1,)), ((), ())), preferred_element_type=F32)


def _rms(x, g):
    return x * lax.rsqrt(jnp.mean(x * x, axis=-1, keepdims=True) + EPS) * g


def _row_spec(tm, n, first=0):
    return pl.BlockSpec((tm, n), lambda i: (first + i, 0))


def _col_spec(n, tm):
    return pl.BlockSpec((n, tm), lambda i: (0, i))


def _full_spec(shape):
    return pl.BlockSpec(shape, lambda *_: (0,) * len(shape))


def _norm_kernel(x_ref, g_ref, o_ref):
    o_ref[...] = _rms(x_ref[...], g_ref[...]).astype(o_ref.dtype)


def _norm(x, g):
    t, d = x.shape
    return pl.pallas_call(
        _norm_kernel,
        grid=(t // TM,),
        in_specs=[_row_spec(TM, d), _full_spec((1, d))],
        out_specs=_row_spec(TM, d),
        out_shape=jax.ShapeDtypeStruct((t, d), BF16),
        compiler_params=_params(("parallel",), 32),
        name="mix_norm",
    )(x, g)


def _glu_kernel(h_ref, wa_ref, wb_ref, o_ref):
    h = h_ref[...]
    o_ref[...] = _dot(h, wa_ref[...]) * jax.nn.sigmoid(_dot(h, wb_ref[...]))


def _glu(h, wa, wb):
    t, d = h.shape
    c = wa.shape[1]
    return pl.pallas_call(
        _glu_kernel,
        grid=(t // TM,),
        in_specs=[_row_spec(TM, d), _full_spec((d, c)), _full_spec((d, c))],
        out_specs=_row_spec(TM, c),
        out_shape=jax.ShapeDtypeStruct((t, c), F32),
        compiler_params=_params(("parallel",), 40),
        name="glu",
    )(h, wa, wb)


def _q_rows_kernel(h_ref, wq_ref, gq_ref, wn_ref, wr_ref, wt_ref, cs_ref, sn_ref, o_ref, *, n_heads, qscale):
    ql = _rms(_dot(h_ref[...], wq_ref[...]), gq_ref[...]).astype(BF16)
    qn = _dot(ql, wn_ref[...]) * qscale
    qr = _dot(ql, wr_ref[...])
    qt = _dot(ql, wt_ref[...])
    cs = cs_ref[...]
    sn = sn_ref[...]
    for h in range(n_heads):
        lo = h * LANES
        o_ref[:, 2 * lo:2 * lo + LANES] = qn[:, lo:lo + LANES].astype(o_ref.dtype)
        rot = qr[:, lo:lo + LANES] * cs + qt[:, lo:lo + LANES] * sn
        o_ref[:, 2 * lo + LANES:2 * lo + 2 * LANES] = rot.astype(o_ref.dtype)


def _q_rows(h, first_tile, n_rows, wq, gq, wn, wr, wt, cs, sn, n_heads, qscale):
    d = h.shape[1]
    ql = wq.shape[1]
    hw = wn.shape[1]
    return pl.pallas_call(
        functools.partial(_q_rows_kernel, n_heads=n_heads, qscale=qscale),
        grid=(n_rows // TM,),
        in_specs=[_row_spec(TM, d, first_tile), _full_spec((d, ql)), _full_spec((1, ql)), _full_spec((ql, hw)),
                  _full_spec((ql, hw)), _full_spec((ql, hw)),
                  _row_spec(TM, LANES, first_tile), _row_spec(TM, LANES, first_tile)],
        out_specs=_row_spec(TM, 2 * hw),
        out_shape=jax.ShapeDtypeStruct((n_rows, 2 * hw), BF16),
        compiler_params=_params(("parallel",), 56),
        name="q_rows",
    )(h, wq, gq, wn, wr, wt, cs, sn)


def _q_cols_kernel(h_ref, wq_ref, gq_ref, wn_ref, wr_ref, wt_ref, cs_ref, sn_ref, o_ref, *, n_heads, qscale):
    ql = _rms(_dot(h_ref[...], wq_ref[...]), gq_ref[...])
    ql_t = ql.T.astype(BF16)
    qn = _dot(wn_ref[...], ql_t) * qscale
    qr = _dot(wr_ref[...], ql_t)
    qt = _dot(wt_ref[...], ql_t)
    cs = cs_ref[...]
    sn = sn_ref[...]
    for h in range(n_heads):
        lo = h * LANES
        o_ref[2 * lo:2 * lo + LANES, :] = qn[lo:lo + LANES, :].astype(o_ref.dtype)
        rot = qr[lo:lo + LANES, :] * cs + qt[lo:lo + LANES, :] * sn
        o_ref[2 * lo + LANES:2 * lo + 2 * LANES, :] = rot.astype(o_ref.dtype)


def _q_cols(h, n_rows, wq, gq, wn_t, wr_t, wt_t, cs_t, sn_t, n_heads, qscale):
    d = h.shape[1]
    ql = wq.shape[1]
    hw = wn_t.shape[0]
    return pl.pallas_call(
        functools.partial(_q_cols_kernel, n_heads=n_heads, qscale=qscale),
        grid=(n_rows // TM,),
        in_specs=[_row_spec(TM, d), _full_spec((d, ql)), _full_spec((1, ql)), _full_spec((hw, ql)),
                  _full_spec((hw, ql)), _full_spec((hw, ql)), _col_spec(LANES, TM), _col_spec(LANES, TM)],
        out_specs=_col_spec(2 * hw, TM),
        out_shape=jax.ShapeDtypeStruct((2 * hw, n_rows), BF16),
        compiler_params=_params(("parallel",), 56),
        name="q_cols",
    )(h, wq, gq, wn_t, wr_t, wt_t, cs_t, sn_t)


def _kv_kernel(h_ref, wkv_ref, gkv_ref, wkr_ref, wkt_ref, cs_ref, sn_ref, *rest, n_heads, rope_dim, v_rows, prompt):
    if prompt:
        wuk_ref, wuv_ref, ckv_ref, kr_ref, kcat_ref, vt_ref = rest
    else:
        ckv_ref, kr_ref = rest
    h = h_ref[...]
    ckv = _rms(_dot(h, wkv_ref[...]), gkv_ref[...])
    ckv_ref[...] = ckv
    kr = _dot(h, wkr_ref[...]) * cs_ref[...] + _dot(h, wkt_ref[...]) * sn_ref[...]
    kr_ref[...] = kr[:, :rope_dim]
    if not prompt:
        return
    kr_b = kr.astype(kcat_ref.dtype)
    kn = _dot(ckv.astype(BF16), wuk_ref[...]).astype(kcat_ref.dtype)
    vt = _dot(wuv_ref[...], ckv.T.astype(BF16)).astype(vt_ref.dtype)
    tm = h.shape[0]
    ones_rows = jnp.where(lax.broadcasted_iota(jnp.int32, (v_rows - LANES, tm), 0) == 0, 1.0, 0.0).astype(vt_ref.dtype)
    for hd in range(n_heads):
        lo = hd * LANES
        kcat_ref[:, 2 * lo:2 * lo + LANES] = kn[:, lo:lo + LANES]
        kcat_ref[:, 2 * lo + LANES:2 * lo + 2 * LANES] = kr_b
        vt_ref[hd * v_rows:hd * v_rows + LANES, :] = vt[lo:lo + LANES, :]
        vt_ref[hd * v_rows + LANES:(hd + 1) * v_rows, :] = ones_rows


def _kv_proj(h, first_tile, n_rows, wkv, gkv, wkr, wkt, cs, sn, n_heads, rope_dim, wuk=None, wuv_t=None):
    d = h.shape[1]
    kl = wkv.shape[1]
    prompt = wuk is not None
    v_rows = LANES + BF16_ROWS
    in_specs = [_row_spec(TM, d, first_tile), _full_spec((d, kl)), _full_spec((1, kl)), _full_spec((d, LANES)),
                _full_spec((d, LANES)), _row_spec(TM, LANES, first_tile), _row_spec(TM, LANES, first_tile)]
    out_specs = [_row_spec(TM, kl), _row_spec(TM, rope_dim)]
    out_shape = [jax.ShapeDtypeStruct((n_rows, kl), F32), jax.ShapeDtypeStruct((n_rows, rope_dim), F32)]
    args = [h, wkv, gkv, wkr, wkt, cs, sn]
    if prompt:
        hw = wuk.shape[1]
        in_specs += [_full_spec((kl, hw)), _full_spec((hw, kl))]
        out_specs += [_row_spec(TM, 2 * hw), _col_spec(n_heads * v_rows, TM)]
        out_shape += [jax.ShapeDtypeStruct((n_rows, 2 * hw), BF16),
                      jax.ShapeDtypeStruct((n_heads * v_rows, n_rows), BF16)]
        args += [wuk, wuv_t]
    return pl.pallas_call(
        functools.partial(_kv_kernel, n_heads=n_heads, rope_dim=rope_dim, v_rows=v_rows, prompt=prompt),
        grid=(n_rows // TM,),
        in_specs=in_specs,
        out_specs=out_specs,
        out_shape=out_shape,
        compiler_params=_params(("parallel",), 56),
        name="kv_proj_prompt" if prompt else "kv_proj_sample",
    )(*args)


def _gate_kernel(h_ref, w_ref, b_ref, o_ref):
    o_ref[...] = jax.nn.sigmoid(_dot(h_ref[...], w_ref[...]) + b_ref[...]).astype(o_ref.dtype)


def _gates(h, w, b, tn=1024):
    t, d = h.shape
    n = w.shape[1]
    return pl.pallas_call(
        _gate_kernel,
        grid=(t // TM, n // tn),
        in_specs=[pl.BlockSpec((TM, d), lambda i, j: (i, 0)), pl.BlockSpec((d, tn), lambda i, j: (0, j)),
                  pl.BlockSpec((1, tn), lambda i, j: (0, j))],
        out_specs=pl.BlockSpec((TM, tn), lambda i, j: (i, j)),
        out_shape=jax.ShapeDtypeStruct((t, n), BF16),
        compiler_params=_params(("parallel", "arbitrary"), 40),
        name="branch_gates",
    )(h, w, b)


def _ln_silu(y, g, b):
    mu = jnp.mean(y, axis=-1, keepdims=True)
    d = y - mu
    var = jnp.mean(d * d, axis=-1, keepdims=True)
    z = d * lax.rsqrt(var + EPS) * g + b
    return z * jax.nn.sigmoid(z)


def _conv_prompt_kernel(prev_ref, cur_ref, w_ref, bdw_ref, g_ref, b_ref, o_ref, ext_ref, y_ref, *, width):
    i = pl.program_id(0)
    tm, ch = cur_ref.shape

    @pl.when(i == 0)
    def _():
        ext_ref[0:CONV_HALO, :] = jnp.zeros((CONV_HALO, ch), F32)

    @pl.when(i > 0)
    def _():
        ext_ref[0:CONV_HALO, :] = prev_ref[...]

    ext_ref[CONV_HALO:CONV_HALO + tm, :] = cur_ref[...]

    lead = CONV_HALO - (width - 1)
    by_shift = {}
    for k in range(width):
        a, s = divmod(lead + k, 8)
        by_shift.setdefault(s, []).append((a, k))
    for r0 in range(0, tm, CONV_ROWS):
        for c0 in range(0, ch, LANES):
            acc = jnp.zeros((CONV_ROWS, LANES), F32) + bdw_ref[:, c0:c0 + LANES]
            for s, taps in by_shift.items():
                a_lo = min(a for a, _ in taps)
                a_hi = max(a for a, _ in taps)
                win = ext_ref[r0 + s + 8 * a_lo:r0 + s + 8 * a_hi + CONV_ROWS, c0:c0 + LANES]
                for a, k in taps:
                    off = 8 * (a - a_lo)
                    acc = acc + win[off:off + CONV_ROWS, :] * w_ref[k:k + 1, c0:c0 + LANES]
            y_ref[r0:r0 + CONV_ROWS, c0:c0 + LANES] = acc
    o_ref[...] = _ln_silu(y_ref[...], g_ref[...], b_ref[...]).astype(o_ref.dtype)


def _conv_prompt(u_all, t_prompt, w_dw, b_dw, g_ln, b_ln):
    t_all, ch = u_all.shape
    width = w_dw.shape[0]
    per = TM_CONV // CONV_HALO
    return pl.pallas_call(
        functools.partial(_conv_prompt_kernel, width=width),
        grid=(t_prompt // TM_CONV,),
        in_specs=[pl.BlockSpec((CONV_HALO, ch), lambda i: (jnp.maximum(i * per - 1, 0), 0)),
                  _row_spec(TM_CONV, ch), _full_spec((width, ch)), _full_spec((1, ch)),
                  _full_spec((1, ch)), _full_spec((1, ch))],
        out_specs=_row_spec(TM_CONV, ch),
        out_shape=jax.ShapeDtypeStruct((t_all, ch), BF16),
        scratch_shapes=[pltpu.VMEM((TM_CONV + CONV_HALO, ch), F32), pltpu.VMEM((TM_CONV, ch), F32)],
        compiler_params=_params(("arbitrary",), 32),
        name="conv_prompt",
    )(u_all, u_all, w_dw, b_dw, g_ln, b_ln)


def _conv_sample_kernel(st_ref, u_ref, w_ref, bdw_ref, g_ref, b_ref, o_ref):
    n_state = st_ref.shape[0]
    n_new = u_ref.shape[0]
    width = w_ref.shape[0]
    for t in range(n_new):
        acc = jnp.zeros(o_ref.shape[1:], F32) + bdw_ref[...]
        for k in range(width):
            j = t + k
            src = st_ref[j] if j < n_state else u_ref[j - n_state]
            acc = acc + src * w_ref[k:k + 1, :]
        o_ref[t] = _ln_silu(acc, g_ref[...], b_ref[...]).astype(o_ref.dtype)


def _conv_sample(state_t, u_t, w_dw, b_dw, g_ln, b_ln):
    n_state, nb, ch = state_t.shape
    n_new = u_t.shape[0]
    width = w_dw.shape[0]
    assert width == n_state + 1
    return pl.pallas_call(
        _conv_sample_kernel,
        grid=(nb // SEQ_PER_STEP,),
        in_specs=[pl.BlockSpec((n_state, SEQ_PER_STEP, ch), lambda i: (0, i, 0)),
                  pl.BlockSpec((n_new, SEQ_PER_STEP, ch), lambda i: (0, i, 0)),
                  _full_spec((width, ch)), _full_spec((1, ch)), _full_spec((1, ch)), _full_spec((1, ch))],
        out_specs=pl.BlockSpec((n_new, SEQ_PER_STEP, ch), lambda i: (0, i, 0)),
        out_shape=jax.ShapeDtypeStruct((n_new, nb, ch), BF16),
        compiler_params=_params(("parallel",), 32),
        name="conv_sample",
    )(state_t, u_t, w_dw, b_dw, g_ln, b_ln)


def _flash_kernel(qt_ref, k_ref, vt_ref, o_ref):
    qi = pl.program_id(1)
    tq = qt_ref.shape[1]
    dq = qt_ref.shape[0] // FLASH_HEADS
    v_rows = vt_ref.shape[0] // FLASH_HEADS
    dv = o_ref.shape[1] // FLASH_HEADS

    def block(off, carry, first_key):
        out = []
        for hd in range(FLASH_HEADS):
            m, acc = carry[hd]
            s = _dot(k_ref[pl.ds(off, TK), hd * dq:(hd + 1) * dq], qt_ref[hd * dq:(hd + 1) * dq, :])
            if first_key is not None:
                key = first_key + lax.broadcasted_iota(jnp.int32, (TK, tq), 0)
                qry = lax.broadcasted_iota(jnp.int32, (TK, tq), 1)
                s = jnp.where(key <= qry, s, -jnp.inf)
            m_new = jnp.maximum(m, jnp.max(s, axis=0, keepdims=True))
            p = jnp.exp2(s - m_new).astype(vt_ref.dtype)
            pv = _dot(vt_ref[hd * v_rows:(hd + 1) * v_rows, pl.ds(off, TK)], p)
            out.append((m_new, jnp.exp2(m - m_new) * acc + pv))
        return tuple(out)

    per = tq // TK
    init = tuple((jnp.full((1, tq), -jnp.inf, F32), jnp.zeros((v_rows, tq), F32)) for _ in range(FLASH_HEADS))
    carry = lax.fori_loop(0, qi * per, lambda kb, c: block(pl.multiple_of(kb * TK, TK), c, None), init)
    for j in range(per):
        carry = block(pl.multiple_of(qi * tq + j * TK, TK), carry, j * TK)
    for hd in range(FLASH_HEADS):
        acc = carry[hd][1]
        o_ref[:, hd * dv:(hd + 1) * dv] = (acc[:dv, :] / acc[dv:dv + 1, :]).T.astype(o_ref.dtype)


def _flash(q_t, kcat, v_t, t_all, n_heads):
    t_prompt = q_t.shape[1]
    dq = q_t.shape[0] // n_heads
    v_rows = v_t.shape[0] // n_heads
    nh = FLASH_HEADS
    assert n_heads % nh == 0
    return pl.pallas_call(
        _flash_kernel,
        grid=(n_heads // nh, t_prompt // TQ),
        in_specs=[pl.BlockSpec((nh * dq, TQ), lambda h, i: (h, i)),
                  pl.BlockSpec((t_prompt, nh * dq), lambda h, i: (0, h)),
                  pl.BlockSpec((nh * v_rows, t_prompt), lambda h, i: (h, 0))],
        out_specs=pl.BlockSpec((TQ, nh * LANES), lambda h, i: (i, h)),
        out_shape=jax.ShapeDtypeStruct((t_all, n_heads * LANES), BF16),
        compiler_params=_params(("parallel", "arbitrary"), 56),
        name="prompt_attention",
    )(q_t, kcat, v_t)


def _absorb_kernel(q_ref, w_ref, o_ref):
    o_ref[...] = _dot(q_ref[...], w_ref[...]).astype(o_ref.dtype)


def _absorb(qcat_s, wuk_t):
    n_heads, dn, kl = wuk_t.shape
    ts = qcat_s.shape[0]
    return pl.pallas_call(
        _absorb_kernel,
        grid=(n_heads,),
        in_specs=[pl.BlockSpec((ts, dn), lambda h: (0, 2 * h)),
                  pl.BlockSpec((None, dn, kl), lambda h: (h, 0, 0))],
        out_specs=pl.BlockSpec((ts, kl), lambda h: (0, h)),
        out_shape=jax.ShapeDtypeStruct((ts, n_heads * kl), BF16),
        compiler_params=_params(("parallel",), 32),
        name="absorb_q",
    )(qcat_s, wuk_t)


def _softmax_step(s, v, m, l, acc):
    m_new = jnp.maximum(m, jnp.max(s, axis=-1, keepdims=True))
    alpha = jnp.exp2(m - m_new)
    p = jnp.exp2(s - m_new)
    l_new = alpha * l + jnp.sum(p, axis=-1, keepdims=True)
    acc_new = alpha * acc + _dot(p.astype(v.dtype), v)
    return m_new, l_new, acc_new


def _sample_attn_kernel(pt_ref, qa_ref, qr_ref, cn_ref, kn_ref, *rest, n_heads, rope_dim):
    del pt_ref
    npg = PAGES_PER_STEP
    kc_refs, kr_refs = rest[:npg], rest[npg:2 * npg]
    o_ref, kc_s, kr_s, m_s, l_s, acc_s = rest[2 * npg:]
    c = pl.program_id(1)
    qa = qa_ref[...]
    qr = qr_ref[:, :rope_dim]
    rows = qa.shape[0]

    @pl.when(c == 0)
    def _():
        qa32 = qa.astype(F32)
        qr32 = qr.astype(F32)
        n_new = cn_ref.shape[0]
        tok = lax.broadcasted_iota(jnp.int32, (rows, 1), 0) // n_heads
        scores = []
        for j in range(n_new):
            sj = (jnp.sum(qa32 * cn_ref[j:j + 1, :], axis=-1, keepdims=True)
                  + jnp.sum(qr32 * kn_ref[j:j + 1, :], axis=-1, keepdims=True))
            scores.append(jnp.where(tok >= j, sj, -jnp.inf))
        m = scores[0]
        for sj in scores[1:]:
            m = jnp.maximum(m, sj)
        l = jnp.zeros_like(m)
        acc = jnp.zeros(acc_s.shape, F32)
        for j, sj in enumerate(scores):
            pj = jnp.exp2(sj - m)
            l = l + pj
            acc = acc + pj * cn_ref[j:j + 1, :]
        m_s[...] = m
        l_s[...] = l
        acc_s[...] = acc

    per = npg // PAGE_GROUPS
    carry = (m_s[...], l_s[...], acc_s[...])
    for g in range(PAGE_GROUPS):
        lo = g * per * PAGE_SIZE
        for i in range(per):
            at = lo + i * PAGE_SIZE
            kc_s[at:at + PAGE_SIZE, :] = kc_refs[g * per + i][...].astype(kc_s.dtype)
            kr_s[:, at:at + PAGE_SIZE] = kr_refs[g * per + i][...].astype(kr_s.dtype)
        kc = kc_s[lo:lo + per * PAGE_SIZE, :]
        s = _dot_t(qa, kc) + _dot(qr, kr_s[:, lo:lo + per * PAGE_SIZE])
        carry = _softmax_step(s, kc, *carry)
    m, l, acc = carry
    m_s[...] = m
    l_s[...] = l
    acc_s[...] = acc

    @pl.when(c == pl.num_programs(1) - 1)
    def _():
        o_ref[...] = (acc / l).astype(o_ref.dtype)


def _sample_attn(page_table, qabs, qcat_s, ckv_s, kr_s, pool_c, pool_r_t, n_heads):
    nb, n_pages = page_table.shape
    kl = pool_c.shape[-1]
    rope_dim = pool_r_t.shape[-2]
    ts = qabs.shape[0]
    n_new = ts // nb
    rows = n_new * n_heads
    npg = PAGES_PER_STEP
    dq = qcat_s.shape[1] // n_heads
    qa3 = qabs.reshape(nb, rows, kl)
    qc3 = qcat_s.reshape(nb, rows, dq)
    cn3 = ckv_s.reshape(nb, n_new, kl)
    kn3 = kr_s.reshape(nb, n_new, rope_dim)
    pt = page_table.reshape(-1)

    def page_spec(shape, i):
        return pl.BlockSpec((None, None) + shape,
                            lambda b, c, pt_ref: (0, pt_ref[b * n_pages + c * npg + i], 0, 0))

    in_specs = ([pl.BlockSpec((None, rows, kl), lambda b, c, pt_ref: (b, 0, 0)),
                 pl.BlockSpec((None, rows, LANES), lambda b, c, pt_ref: (b, 0, 1)),
                 pl.BlockSpec((None, n_new, kl), lambda b, c, pt_ref: (b, 0, 0)),
                 pl.BlockSpec((None, n_new, rope_dim), lambda b, c, pt_ref: (b, 0, 0))]
                + [page_spec((PAGE_SIZE, kl), i) for i in range(npg)]
                + [page_spec((rope_dim, PAGE_SIZE), i) for i in range(npg)])
    grid_spec = pltpu.PrefetchScalarGridSpec(
        num_scalar_prefetch=1,
        grid=(nb, n_pages // npg),
        in_specs=in_specs,
        out_specs=pl.BlockSpec((None, rows, kl), lambda b, c, pt_ref: (b, 0, 0)),
        scratch_shapes=[pltpu.VMEM((npg * PAGE_SIZE, kl), BF16), pltpu.VMEM((rope_dim, npg * PAGE_SIZE), BF16),
                        pltpu.VMEM((rows, 1), F32), pltpu.VMEM((rows, 1), F32), pltpu.VMEM((rows, kl), F32)],
    )
    return pl.pallas_call(
        functools.partial(_sample_attn_kernel, n_heads=n_heads, rope_dim=rope_dim),
        grid_spec=grid_spec,
        out_shape=jax.ShapeDtypeStruct((nb, rows, kl), BF16),
        compiler_params=_params(("parallel", "arbitrary"), 40),
        name="sample_attention",
    )(pt, qa3, qc3, cn3, kn3, *([pool_c] * npg), *([pool_r_t] * npg))


def _uv_kernel(y_hbm_ref, o_ref_in, w_ref, out_ref):
    del y_hbm_ref
    out_ref[...] = _dot(o_ref_in[...], w_ref[...]).astype(out_ref.dtype)


def _uv(y_attn, olat, wuv_h, t_prompt):
    n_heads, kl, dv = wuv_h.shape
    ts = olat.shape[0]
    return pl.pallas_call(
        _uv_kernel,
        grid=(n_heads,),
        in_specs=[pl.BlockSpec(memory_space=pl.ANY),
                  pl.BlockSpec((ts, kl), lambda h: (0, h)),
                  pl.BlockSpec((None, kl, dv), lambda h: (h, 0, 0))],
        out_specs=pl.BlockSpec((ts, dv), lambda h: (t_prompt // ts, h)),
        out_shape=jax.ShapeDtypeStruct(y_attn.shape, y_attn.dtype),
        input_output_aliases={0: 0},
        compiler_params=_params(("arbitrary",), 32),
        name="value_up",
    )(y_attn, olat, wuv_h)


def _out_kernel(x_ref, z_ref, ya_ref, g_ref, wc_ref, wo_ref, gn_ref, xo_ref, ho_ref):
    d = x_ref.shape[1]
    yc = _dot(z_ref[...], wc_ref[...])
    mix = g_ref[:, :d].astype(F32) * yc + g_ref[:, d:].astype(F32) * ya_ref[...].astype(F32)
    x1 = x_ref[...] + _dot(mix.astype(BF16), wo_ref[...])
    xo_ref[...] = x1
    ho_ref[...] = _rms(x1, gn_ref[...]).astype(ho_ref.dtype)


def _out_proj(x, z, y_attn, gates, w_conv_out, w_out, g_ffn):
    t, d = x.shape
    ch = z.shape[1]
    return pl.pallas_call(
        _out_kernel,
        grid=(t // TM_OUT,),
        in_specs=[_row_spec(TM_OUT, d), _row_spec(TM_OUT, ch), _row_spec(TM_OUT, d), _row_spec(TM_OUT, 2 * d),
                  _full_spec((ch, d)), _full_spec((d, d)), _full_spec((1, d))],
        out_specs=[_row_spec(TM_OUT, d), _row_spec(TM_OUT, d)],
        out_shape=[jax.ShapeDtypeStruct((t, d), F32), jax.ShapeDtypeStruct((t, d), BF16)],
        compiler_params=_params(("parallel",), 56),
        name="out_proj",
    )(x, z, y_attn, gates, w_conv_out, w_out, g_ffn)


def _ffn_kernel(h_ref, x_ref, wg_ref, wu_ref, wd_ref, gn_ref, xo_ref, ho_ref, acc_ref):
    f = pl.program_id(1)

    @pl.when(f == 0)
    def _():
        acc_ref[...] = x_ref[...]

    h = h_ref[...]
    g = _dot(h, wg_ref[...])
    a = (g * jax.nn.sigmoid(g) * _dot(h, wu_ref[...])).astype(BF16)
    acc_ref[...] += _dot(a, wd_ref[...])

    @pl.when(f == pl.num_programs(1) - 1)
    def _():
        x2 = acc_ref[...]
        xo_ref[...] = x2
        ho_ref[...] = _rms(x2, gn_ref[...]).astype(ho_ref.dtype)


def _ffn(h2, x1, wg, wu, wd, g_ple):
    t, d = x1.shape
    dff = wg.shape[1]
    return pl.pallas_call(
        _ffn_kernel,
        grid=(t // TM, dff // TF),
        in_specs=[pl.BlockSpec((TM, d), lambda i, f: (i, 0)), pl.BlockSpec((TM, d), lambda i, f: (i, 0)),
                  pl.BlockSpec((d, TF), lambda i, f: (0, f)), pl.BlockSpec((d, TF), lambda i, f: (0, f)),
                  pl.BlockSpec((TF, d), lambda i, f: (f, 0)), pl.BlockSpec((1, d), lambda i, f: (0, 0))],
        out_specs=[pl.BlockSpec((TM, d), lambda i, f: (i, 0)), pl.BlockSpec((TM, d), lambda i, f: (i, 0))],
        out_shape=[jax.ShapeDtypeStruct((t, d), F32), jax.ShapeDtypeStruct((t, d), BF16)],
        scratch_shapes=[pltpu.VMEM((TM, d), F32)],
        compiler_params=_params(("parallel", "arbitrary"), 56),
        name="swiglu",
    )(h2, x1, wg, wu, wd, g_ple)


def _ple_kernel(h_ref, x_ref, p_ref, wg_ref, wp_ref, gf_ref, y_ref):
    gate = jax.nn.sigmoid(_dot(h_ref[...], wg_ref[...]))
    x3 = x_ref[...] + gate * _dot(p_ref[...].astype(BF16), wp_ref[...])
    y_ref[...] = _rms(x3, gf_ref[...])


def _ple_final(h3, x2, pemb, w_gate, w_ple, g_final, first_tile):
    n, pd = pemb.shape
    d = x2.shape[1]
    return pl.pallas_call(
        _ple_kernel,
        grid=(n // TM,),
        in_specs=[_row_spec(TM, d, first_tile), _row_spec(TM, d, first_tile),
                  _row_spec(TM, pd), _full_spec((d, d)), _full_spec((pd, d)), _full_spec((1, d))],
        out_specs=_row_spec(TM, d),
        out_shape=jax.ShapeDtypeStruct((n, d), F32),
        compiler_params=_params(("parallel",), 48),
        name="ple_final",
    )(h3, x2, pemb, w_gate, w_ple, g_final)


def _rope_tables(pos, half):
    inv = ROPE_THETA ** (-jnp.arange(half, dtype=F32) / half)
    ang = pos[:, None] * inv[None, :]
    cos = jnp.cos(ang)
    sin = jnp.sin(ang)
    pad = jnp.zeros((pos.shape[0], LANES - 2 * half), F32)
    return jnp.concatenate([cos, cos, pad], axis=-1), jnp.concatenate([sin, sin, pad], axis=-1)


def _rot_cols(w, half):
    return jnp.concatenate([-w[..., half:], w[..., :half]], axis=-1)


def _pad_lanes(w):
    return jnp.concatenate([w, jnp.zeros(w.shape[:-1] + (LANES - w.shape[-1],), w.dtype)], axis=-1)


def kernel(x_prompt, x_sample, p_prompt, p_sample, cache_ckv, cache_krope, page_table, state_conv, g_mix_norm, w_in, b_gate, g_q, w_uq, g_kv, w_ukv, w_dw, b_dw, g_conv_ln, b_conv_ln, w_conv_out, w_out, g_ffn_norm, w_ffn_gate, w_ffn_up, w_ffn_down, w_ple, g_ple_norm, w_ple_gate, g_final):
    depth = w_in.shape[0]
    assert depth == 1, "single-layer step only"
    batch, seq, d = x_prompt.shape
    nb, n_new, _ = x_sample.shape
    t_prompt = batch * seq
    t_sample = nb * n_new
    t_all = t_prompt + t_sample
    ch = w_dw.shape[2]
    n_state = state_conv.shape[2]
    q_lora = g_q.shape[1]
    kv_lora = g_kv.shape[1]
    n_heads = w_uq.shape[2]
    rope_dim = cache_krope.shape[-1]
    nope_dim = w_uq.shape[3] - rope_dim
    half = rope_dim // 2
    qscale = LOG2E / math.sqrt(nope_dim + rope_dim)
    assert batch == 1 and seq >= n_state
    assert nope_dim == LANES and w_ukv.shape[3] - nope_dim == LANES and 2 * half == rope_dim <= LANES
    assert t_prompt % TM == 0 and t_sample == TM and t_prompt % TM_CONV == 0 and t_prompt % TQ == 0
    assert nb % SEQ_PER_STEP == 0 and page_table.shape[1] % PAGES_PER_STEP == 0
    assert PAGES_PER_STEP % PAGE_GROUPS == 0 and cache_ckv.shape[2] == PAGE_SIZE
    sample_tile = t_prompt // TM

    off_glu = 2 * ch
    off_q = off_glu + q_lora
    off_kv = off_q + kv_lora
    off_kr = off_kv + rope_dim

    wi = w_in[0]
    wa = wi[:, :ch].astype(BF16)
    wb = wi[:, ch:off_glu].astype(BF16)
    wq = wi[:, off_glu:off_q].astype(BF16)
    wkv = wi[:, off_q:off_kv].astype(BF16)
    wkr = wi[:, off_kv:off_kr]
    wkr_p = _pad_lanes(wkr).astype(BF16)
    wkt_p = _pad_lanes(_rot_cols(wkr, half)).astype(BF16)
    wgt = wi[:, off_kr:].astype(BF16)
    wq_n = w_uq[0][:, :, :nope_dim].astype(BF16)
    wq_rope = w_uq[0][:, :, nope_dim:]
    wq_r = _pad_lanes(wq_rope).astype(BF16)
    wq_t = _pad_lanes(_rot_cols(wq_rope, half)).astype(BF16)
    flat = lambda w: w.reshape(q_lora, n_heads * LANES)
    flat_t = lambda w: jnp.transpose(w, (1, 2, 0)).reshape(n_heads * LANES, q_lora)
    wuk = w_ukv[0][:, :, :nope_dim]
    wuv = w_ukv[0][:, :, nope_dim:]
    wuk_flat = wuk.reshape(kv_lora, n_heads * nope_dim).astype(BF16)
    wuv_flat_t = jnp.transpose(wuv, (1, 2, 0)).reshape(n_heads * LANES, kv_lora).astype(BF16)
    wuk_t = jnp.transpose(wuk, (1, 2, 0)).astype(BF16)
    wuv_h = jnp.transpose(wuv, (1, 0, 2)).astype(BF16)
    row = lambda a: a.reshape(1, -1)

    n_past = page_table.shape[1] * PAGE_SIZE
    pos = jnp.concatenate([jnp.arange(seq, dtype=F32), jnp.tile(n_past + jnp.arange(n_new, dtype=F32), nb)])
    cs, sn = _rope_tables(pos, half)
    cs_q = cs * qscale
    sn_q = sn * qscale

    x_all = jnp.concatenate([x_prompt.reshape(t_prompt, d), x_sample.reshape(t_sample, d)], axis=0)

    h = _norm(x_all, row(g_mix_norm[0]))
    u_all = _glu(h, wa, wb)
    gq = row(g_q[0])
    q_t = _q_cols(h, t_prompt, wq, gq, flat_t(wq_n), flat_t(wq_r), flat_t(wq_t),
                  cs_q[:t_prompt].T, sn_q[:t_prompt].T, n_heads, qscale)
    qcat_s = _q_rows(h, sample_tile, t_sample, wq, gq, flat(wq_n), flat(wq_r), flat(wq_t), cs_q, sn_q,
                     n_heads, qscale)
    gkv = row(g_kv[0])
    ckv_p, kr_p, kcat, v_t = _kv_proj(h, 0, t_prompt, wkv, gkv, wkr_p, wkt_p, cs, sn, n_heads, rope_dim,
                                      wuk_flat, wuv_flat_t)
    ckv_s, kr_s = _kv_proj(h, sample_tile, t_sample, wkv, gkv, wkr_p, wkt_p, cs, sn, n_heads, rope_dim)
    gates = _gates(h, wgt, row(b_gate[0]))

    wdw = w_dw[0]
    conv_args = (wdw, row(b_dw[0]), row(g_conv_ln[0]), row(b_conv_ln[0]))
    z = _conv_prompt(u_all, t_prompt, *conv_args)
    state_t = jnp.transpose(state_conv[0], (1, 0, 2))
    u_t = jnp.transpose(u_all[t_prompt:].reshape(nb, n_new, ch), (1, 0, 2))
    z_s = _conv_sample(state_t, u_t, *conv_args)
    z = lax.dynamic_update_slice(z, jnp.transpose(z_s, (1, 0, 2)).reshape(t_sample, ch), (t_prompt, 0))

    y_attn = _flash(q_t, kcat, v_t, t_all, n_heads)
    qabs = _absorb(qcat_s, wuk_t)
    olat = _sample_attn(page_table, qabs, qcat_s, ckv_s, kr_s, cache_ckv, jnp.swapaxes(cache_krope, 2, 3), n_heads)
    y_attn = _uv(y_attn, olat.reshape(t_sample, n_heads * kv_lora), wuv_h, t_prompt)

    x1, h2 = _out_proj(x_all, z, y_attn, gates, w_conv_out[0].astype(BF16), w_out[0].astype(BF16),
                       row(g_ffn_norm[0]))
    x2, h3 = _ffn(h2, x1, w_ffn_gate[0].astype(BF16), w_ffn_up[0].astype(BF16), w_ffn_down[0].astype(BF16),
                  row(g_ple_norm[0]))
    wpg = w_ple_gate[0].astype(BF16)
    wp = w_ple[0].astype(BF16)
    gf = row(g_final)
    y_prompt = _ple_final(h3, x2, p_prompt[0].reshape(t_prompt, -1), wpg, wp, gf, 0)
    y_sample = _ple_final(h3, x2, p_sample[0].reshape(t_sample, -1), wpg, wp, gf, sample_tile)

    conv_p = u_all[t_prompt - n_state:t_prompt].reshape(1, batch, n_state, ch)
    conv_s = jnp.transpose(jnp.concatenate([state_t[n_new:], u_t], axis=0), (1, 0, 2))[None]
    return (y_prompt.reshape(batch, seq, d), y_sample.reshape(nb, n_new, d),
            ckv_p.reshape(1, batch, seq, kv_lora), kr_p.reshape(1, batch, seq, rope_dim), conv_p,
            ckv_s.reshape(1, nb, n_new, kv_lora), kr_s.reshape(1, nb, n_new, rope_dim), conv_s)
```

```python
import functools
import math

import jax
import jax.numpy as jnp
from jax import lax
from jax.experimental import pallas as pl
from jax.experimental.pallas import tpu as pltpu

F32 = jnp.float32
BF16 = jnp.bfloat16

EPS = 1e-6
ROPE_THETA = 10000.0
PAGE_SIZE = 128
LANES = 128
BF16_ROWS = 16
MIB = 1024 * 1024
LOG2E = math.log2(math.e)

TM = 512
TM_OUT = 256
TM_CONV = 256
CONV_ROWS = 64
CONV_HALO = 32
SEQ_PER_STEP = 32
TQ = 1024
TK = 512
FLASH_HEADS = 2
PAGES_PER_CHUNK = 16
TF = 512


def _params(semantics, vmem_mib):
    return pltpu.CompilerParams(dimension_semantics=semantics, vmem_limit_bytes=vmem_mib * MIB)


def _dot(a, b):
    return jnp.dot(a, b, preferred_element_type=F32)


def _dot_t(a, b):
    return lax.dot_general(a, b, (((1,), (1,)), ((), ())), preferred_element_type=F32)


def _rms(x, g):
    return x * lax.rsqrt(jnp.mean(x * x, axis=-1, keepdims=True) + EPS) * g


def _row_spec(tm, n):
    return pl.BlockSpec((tm, n), lambda i: (i, 0))


def _col_spec(n, tm):
    return pl.BlockSpec((n, tm), lambda i: (0, i))


def _full_spec(shape):
    return pl.BlockSpec(shape, lambda *_: (0,) * len(shape))


def _norm_kernel(x_ref, g_ref, o_ref):
    o_ref[...] = _rms(x_ref[...], g_ref[...]).astype(o_ref.dtype)


def _norm(x, g):
    t, d = x.shape
    return pl.pallas_call(
        _norm_kernel,
        grid=(t // TM,),
        in_specs=[_row_spec(TM, d), _full_spec((1, d))],
        out_specs=_row_spec(TM, d),
        out_shape=jax.ShapeDtypeStruct((t, d), BF16),
        compiler_params=_params(("parallel",), 32),
        name="mix_norm",
    )(x, g)


def _glu_kernel(h_ref, wa_ref, wb_ref, o_ref):
    h = h_ref[...]
    o_ref[...] = _dot(h, wa_ref[...]) * jax.nn.sigmoid(_dot(h, wb_ref[...]))


def _glu(h, wa, wb):
    t, d = h.shape
    c = wa.shape[1]
    return pl.pallas_call(
        _glu_kernel,
        grid=(t // TM,),
        in_specs=[_row_spec(TM, d), _full_spec((d, c)), _full_spec((d, c))],
        out_specs=_row_spec(TM, c),
        out_shape=jax.ShapeDtypeStruct((t, c), F32),
        compiler_params=_params(("parallel",), 40),
        name="glu",
    )(h, wa, wb)


def _q_rows_kernel(h_ref, wq_ref, gq_ref, wn_ref, wr_ref, wt_ref, cs_ref, sn_ref, o_ref, *, n_heads, qscale):
    ql = _rms(_dot(h_ref[...], wq_ref[...]), gq_ref[...]).astype(BF16)
    qn = _dot(ql, wn_ref[...]) * qscale
    qr = _dot(ql, wr_ref[...])
    qt = _dot(ql, wt_ref[...])
    cs = cs_ref[...]
    sn = sn_ref[...]
    for h in range(n_heads):
        lo = h * LANES
        o_ref[:, 2 * lo:2 * lo + LANES] = qn[:, lo:lo + LANES].astype(o_ref.dtype)
        rot = qr[:, lo:lo + LANES] * cs + qt[:, lo:lo + LANES] * sn
        o_ref[:, 2 * lo + LANES:2 * lo + 2 * LANES] = rot.astype(o_ref.dtype)


def _q_rows(h, wq, gq, wn, wr, wt, cs, sn, n_heads, qscale):
    n_rows, d = h.shape
    ql = wq.shape[1]
    hw = wn.shape[1]
    return pl.pallas_call(
        functools.partial(_q_rows_kernel, n_heads=n_heads, qscale=qscale),
        grid=(n_rows // TM,),
        in_specs=[_row_spec(TM, d), _full_spec((d, ql)), _full_spec((1, ql)), _full_spec((ql, hw)),
                  _full_spec((ql, hw)), _full_spec((ql, hw)), _row_spec(TM, LANES), _row_spec(TM, LANES)],
        out_specs=_row_spec(TM, 2 * hw),
        out_shape=jax.ShapeDtypeStruct((n_rows, 2 * hw), BF16),
        compiler_params=_params(("parallel",), 56),
        name="q_rows",
    )(h, wq, gq, wn, wr, wt, cs, sn)


def _q_cols_kernel(h_ref, wq_ref, gq_ref, wn_ref, wr_ref, wt_ref, cs_ref, sn_ref, o_ref, *, n_heads, qscale):
    ql = _rms(_dot(h_ref[...], wq_ref[...]), gq_ref[...])
    ql_t = ql.T.astype(BF16)
    qn = _dot(wn_ref[...], ql_t) * qscale
    qr = _dot(wr_ref[...], ql_t)
    qt = _dot(wt_ref[...], ql_t)
    cs = cs_ref[...]
    sn = sn_ref[...]
    for h in range(n_heads):
        lo = h * LANES
        o_ref[2 * lo:2 * lo + LANES, :] = qn[lo:lo + LANES, :].astype(o_ref.dtype)
        rot = qr[lo:lo + LANES, :] * cs + qt[lo:lo + LANES, :] * sn
        o_ref[2 * lo + LANES:2 * lo + 2 * LANES, :] = rot.astype(o_ref.dtype)


def _q_cols(h, wq, gq, wn_t, wr_t, wt_t, cs_t, sn_t, n_heads, qscale):
    n_rows, d = h.shape
    ql = wq.shape[1]
    hw = wn_t.shape[0]
    return pl.pallas_call(
        functools.partial(_q_cols_kernel, n_heads=n_heads, qscale=qscale),
        grid=(n_rows // TM,),
        in_specs=[_row_spec(TM, d), _full_spec((d, ql)), _full_spec((1, ql)), _full_spec((hw, ql)),
                  _full_spec((hw, ql)), _full_spec((hw, ql)), _col_spec(LANES, TM), _col_spec(LANES, TM)],
        out_specs=_col_spec(2 * hw, TM),
        out_shape=jax.ShapeDtypeStruct((2 * hw, n_rows), BF16),
        compiler_params=_params(("parallel",), 56),
        name="q_cols",
    )(h, wq, gq, wn_t, wr_t, wt_t, cs_t, sn_t)


def _kv_kernel(h_ref, wkv_ref, gkv_ref, wkr_ref, wkt_ref, cs_ref, sn_ref, *rest, n_heads, rope_dim, v_rows, prompt):
    if prompt:
        wuk_ref, wuv_ref, ckv_ref, kr_ref, kcat_ref, vt_ref = rest
    else:
        ckv_ref, kr_ref = rest
    h = h_ref[...]
    ckv = _rms(_dot(h, wkv_ref[...]), gkv_ref[...])
    ckv_ref[...] = ckv
    kr = _dot(h, wkr_ref[...]) * cs_ref[...] + _dot(h, wkt_ref[...]) * sn_ref[...]
    kr_ref[...] = kr[:, :rope_dim]
    if not prompt:
        return
    kr_b = kr.astype(kcat_ref.dtype)
    kn = _dot(ckv.astype(BF16), wuk_ref[...]).astype(kcat_ref.dtype)
    vt = _dot(wuv_ref[...], ckv.T.astype(BF16)).astype(vt_ref.dtype)
    tm = h.shape[0]
    ones_rows = jnp.where(lax.broadcasted_iota(jnp.int32, (v_rows - LANES, tm), 0) == 0, 1.0, 0.0).astype(vt_ref.dtype)
    for hd in range(n_heads):
        lo = hd * LANES
        kcat_ref[:, 2 * lo:2 * lo + LANES] = kn[:, lo:lo + LANES]
        kcat_ref[:, 2 * lo + LANES:2 * lo + 2 * LANES] = kr_b
        vt_ref[hd * v_rows:hd * v_rows + LANES, :] = vt[lo:lo + LANES, :]
        vt_ref[hd * v_rows + LANES:(hd + 1) * v_rows, :] = ones_rows


def _kv_proj(h, wkv, gkv, wkr, wkt, cs, sn, n_heads, rope_dim, wuk=None, wuv_t=None):
    n_rows, d = h.shape
    kl = wkv.shape[1]
    prompt = wuk is not None
    v_rows = LANES + BF16_ROWS
    in_specs = [_row_spec(TM, d), _full_spec((d, kl)), _full_spec((1, kl)), _full_spec((d, LANES)),
                _full_spec((d, LANES)), _row_spec(TM, LANES), _row_spec(TM, LANES)]
    out_specs = [_row_spec(TM, kl), _row_spec(TM, rope_dim)]
    out_shape = [jax.ShapeDtypeStruct((n_rows, kl), F32), jax.ShapeDtypeStruct((n_rows, rope_dim), F32)]
    args = [h, wkv, gkv, wkr, wkt, cs, sn]
    if prompt:
        hw = wuk.shape[1]
        in_specs += [_full_spec((kl, hw)), _full_spec((hw, kl))]
        out_specs += [_row_spec(TM, 2 * hw), _col_spec(n_heads * v_rows, TM)]
        out_shape += [jax.ShapeDtypeStruct((n_rows, 2 * hw), BF16),
                      jax.ShapeDtypeStruct((n_heads * v_rows, n_rows), BF16)]
        args += [wuk, wuv_t]
    return pl.pallas_call(
        functools.partial(_kv_kernel, n_heads=n_heads, rope_dim=rope_dim, v_rows=v_rows, prompt=prompt),
        grid=(n_rows // TM,),
        in_specs=in_specs,
        out_specs=out_specs,
        out_shape=out_shape,
        compiler_params=_params(("parallel",), 56),
        name="kv_proj_prompt" if prompt else "kv_proj_sample",
    )(*args)


def _gate_kernel(h_ref, w_ref, b_ref, o_ref):
    o_ref[...] = jax.nn.sigmoid(_dot(h_ref[...], w_ref[...]) + b_ref[...]).astype(o_ref.dtype)


def _gates(h, w, b, tn=1024):
    t, d = h.shape
    n = w.shape[1]
    return pl.pallas_call(
        _gate_kernel,
        grid=(t // TM, n // tn),
        in_specs=[pl.BlockSpec((TM, d), lambda i, j: (i, 0)), pl.BlockSpec((d, tn), lambda i, j: (0, j)),
                  pl.BlockSpec((1, tn), lambda i, j: (0, j))],
        out_specs=pl.BlockSpec((TM, tn), lambda i, j: (i, j)),
        out_shape=jax.ShapeDtypeStruct((t, n), BF16),
        compiler_params=_params(("parallel", "arbitrary"), 40),
        name="branch_gates",
    )(h, w, b)


def _ln_silu(y, g, b):
    mu = jnp.mean(y, axis=-1, keepdims=True)
    d = y - mu
    var = jnp.mean(d * d, axis=-1, keepdims=True)
    z = d * lax.rsqrt(var + EPS) * g + b
    return z * jax.nn.sigmoid(z)


def _conv_prompt_kernel(prev_ref, cur_ref, w_ref, bdw_ref, g_ref, b_ref, o_ref, ext_ref, y_ref, *, width):
    i = pl.program_id(0)
    tm, ch = cur_ref.shape

    @pl.when(i == 0)
    def _():
        ext_ref[0:CONV_HALO, :] = jnp.zeros((CONV_HALO, ch), F32)

    @pl.when(i > 0)
    def _():
        ext_ref[0:CONV_HALO, :] = prev_ref[...]

    ext_ref[CONV_HALO:CONV_HALO + tm, :] = cur_ref[...]

    lead = CONV_HALO - (width - 1)
    by_shift = {}
    for k in range(width):
        a, s = divmod(lead + k, 8)
        by_shift.setdefault(s, []).append((a, k))
    for r0 in range(0, tm, CONV_ROWS):
        for c0 in range(0, ch, LANES):
            acc = jnp.zeros((CONV_ROWS, LANES), F32) + bdw_ref[:, c0:c0 + LANES]
            for s, taps in by_shift.items():
                a_lo = min(a for a, _ in taps)
                a_hi = max(a for a, _ in taps)
                win = ext_ref[r0 + s + 8 * a_lo:r0 + s + 8 * a_hi + CONV_ROWS, c0:c0 + LANES]
                for a, k in taps:
                    off = 8 * (a - a_lo)
                    acc = acc + win[off:off + CONV_ROWS, :] * w_ref[k:k + 1, c0:c0 + LANES]
            y_ref[r0:r0 + CONV_ROWS, c0:c0 + LANES] = acc
    o_ref[...] = _ln_silu(y_ref[...], g_ref[...], b_ref[...]).astype(o_ref.dtype)


def _conv_prompt(u, w_dw, b_dw, g_ln, b_ln):
    t, ch = u.shape
    width = w_dw.shape[0]
    per = TM_CONV // CONV_HALO
    return pl.pallas_call(
        functools.partial(_conv_prompt_kernel, width=width),
        grid=(t // TM_CONV,),
        in_specs=[pl.BlockSpec((CONV_HALO, ch), lambda i: (jnp.maximum(i * per - 1, 0), 0)),
                  _row_spec(TM_CONV, ch), _full_spec((width, ch)), _full_spec((1, ch)),
                  _full_spec((1, ch)), _full_spec((1, ch))],
        out_specs=_row_spec(TM_CONV, ch),
        out_shape=jax.ShapeDtypeStruct((t, ch), BF16),
        scratch_shapes=[pltpu.VMEM((TM_CONV + CONV_HALO, ch), F32), pltpu.VMEM((TM_CONV, ch), F32)],
        compiler_params=_params(("arbitrary",), 32),
        name="conv_prompt",
    )(u, u, w_dw, b_dw, g_ln, b_ln)


def _conv_sample_kernel(st_ref, u_ref, w_ref, bdw_ref, g_ref, b_ref, o_ref):
    n_state = st_ref.shape[0]
    n_new = u_ref.shape[0]
    width = w_ref.shape[0]
    for t in range(n_new):
        acc = jnp.zeros(o_ref.shape[1:], F32) + bdw_ref[...]
        for k in range(width):
            j = t + k
            src = st_ref[j] if j < n_state else u_ref[j - n_state]
            acc = acc + src * w_ref[k:k + 1, :]
        o_ref[t] = _ln_silu(acc, g_ref[...], b_ref[...]).astype(o_ref.dtype)


def _conv_sample(state_t, u_t, w_dw, b_dw, g_ln, b_ln):
    n_state, nb, ch = state_t.shape
    n_new = u_t.shape[0]
    width = w_dw.shape[0]
    assert width == n_state + 1
    return pl.pallas_call(
        _conv_sample_kernel,
        grid=(nb // SEQ_PER_STEP,),
        in_specs=[pl.BlockSpec((n_state, SEQ_PER_STEP, ch), lambda i: (0, i, 0)),
                  pl.BlockSpec((n_new, SEQ_PER_STEP, ch), lambda i: (0, i, 0)),
                  _full_spec((width, ch)), _full_spec((1, ch)), _full_spec((1, ch)), _full_spec((1, ch))],
        out_specs=pl.BlockSpec((n_new, SEQ_PER_STEP, ch), lambda i: (0, i, 0)),
        out_shape=jax.ShapeDtypeStruct((n_new, nb, ch), BF16),
        compiler_params=_params(("parallel",), 32),
        name="conv_sample",
    )(state_t, u_t, w_dw, b_dw, g_ln, b_ln)


def _flash_kernel(qt_ref, k_ref, vt_ref, o_ref):
    qi = pl.program_id(1)
    tq = qt_ref.shape[1]
    dq = qt_ref.shape[0] // FLASH_HEADS
    v_rows = vt_ref.shape[0] // FLASH_HEADS
    dv = o_ref.shape[1] // FLASH_HEADS

    def block(off, carry, first_key):
        out = []
        for hd in range(FLASH_HEADS):
            m, acc = carry[hd]
            s = _dot(k_ref[pl.ds(off, TK), hd * dq:(hd + 1) * dq], qt_ref[hd * dq:(hd + 1) * dq, :])
            if first_key is not None:
                key = first_key + lax.broadcasted_iota(jnp.int32, (TK, tq), 0)
                qry = lax.broadcasted_iota(jnp.int32, (TK, tq), 1)
                s = jnp.where(key <= qry, s, -jnp.inf)
            m_new = jnp.maximum(m, jnp.max(s, axis=0, keepdims=True))
            p = jnp.exp2(s - m_new).astype(vt_ref.dtype)
            pv = _dot(vt_ref[hd * v_rows:(hd + 1) * v_rows, pl.ds(off, TK)], p)
            out.append((m_new, jnp.exp2(m - m_new) * acc + pv))
        return tuple(out)

    per = tq // TK
    init = tuple((jnp.full((1, tq), -jnp.inf, F32), jnp.zeros((v_rows, tq), F32)) for _ in range(FLASH_HEADS))
    carry = lax.fori_loop(0, qi * per, lambda kb, c: block(pl.multiple_of(kb * TK, TK), c, None), init)
    for j in range(per):
        carry = block(pl.multiple_of(qi * tq + j * TK, TK), carry, j * TK)
    for hd in range(FLASH_HEADS):
        acc = carry[hd][1]
        o_ref[:, hd * dv:(hd + 1) * dv] = (acc[:dv, :] / acc[dv:dv + 1, :]).T.astype(o_ref.dtype)


def _flash(q_t, kcat, v_t, n_heads):
    t_prompt = q_t.shape[1]
    dq = q_t.shape[0] // n_heads
    v_rows = v_t.shape[0] // n_heads
    nh = FLASH_HEADS
    assert n_heads % nh == 0
    return pl.pallas_call(
        _flash_kernel,
        grid=(n_heads // nh, t_prompt // TQ),
        in_specs=[pl.BlockSpec((nh * dq, TQ), lambda h, i: (h, i)),
                  pl.BlockSpec((t_prompt, nh * dq), lambda h, i: (0, h)),
                  pl.BlockSpec((nh * v_rows, t_prompt), lambda h, i: (h, 0))],
        out_specs=pl.BlockSpec((TQ, nh * LANES), lambda h, i: (i, h)),
        out_shape=jax.ShapeDtypeStruct((t_prompt, n_heads * LANES), BF16),
        compiler_params=_params(("parallel", "arbitrary"), 56),
        name="prompt_attention",
    )(q_t, kcat, v_t)


def _head_dot_kernel(q_ref, w_ref, o_ref):
    o_ref[...] = _dot(q_ref[...], w_ref[...]).astype(o_ref.dtype)


def _absorb(qcat_s, wuk_t):
    n_heads, dn, kl = wuk_t.shape
    ts = qcat_s.shape[0]
    return pl.pallas_call(
        _head_dot_kernel,
        grid=(n_heads,),
        in_specs=[pl.BlockSpec((ts, dn), lambda h: (0, 2 * h)),
                  pl.BlockSpec((None, dn, kl), lambda h: (h, 0, 0))],
        out_specs=pl.BlockSpec((ts, kl), lambda h: (0, h)),
        out_shape=jax.ShapeDtypeStruct((ts, n_heads * kl), BF16),
        compiler_params=_params(("parallel",), 32),
        name="absorb_q",
    )(qcat_s, wuk_t)


def _softmax_step(s, v, m, l, acc):
    m_new = jnp.maximum(m, jnp.max(s, axis=-1, keepdims=True))
    alpha = jnp.exp2(m - m_new)
    p = jnp.exp2(s - m_new)
    l_new = alpha * l + jnp.sum(p, axis=-1, keepdims=True)
    acc_new = alpha * acc + _dot(p.astype(v.dtype), v)
    return m_new, l_new, acc_new


def _sample_attn_kernel(pt_ref, qa_ref, qr_ref, cn_ref, kn_ref, poolc_ref, poolr_ref, o_ref,
                        kc_buf, kr_buf, kc_s, kr_s, sem_c, sem_r, *, n_heads, rope_dim, n_pages):
    b = pl.program_id(0)
    n_chunks = n_pages // PAGES_PER_CHUNK
    total = pl.num_programs(0) * n_chunks
    qa = qa_ref[...]
    qr = qr_ref[:, :rope_dim]
    rows = qa.shape[0]

    def chunk_copies(g, slot):
        base = g * PAGES_PER_CHUNK
        out = []
        for i in range(PAGES_PER_CHUNK):
            page = pt_ref[base + i]
            out.append(pltpu.make_async_copy(poolc_ref.at[0, page],
                                             kc_buf.at[slot, pl.ds(i * PAGE_SIZE, PAGE_SIZE)], sem_c.at[slot]))
            out.append(pltpu.make_async_copy(poolr_ref.at[0, page], kr_buf.at[slot, i], sem_r.at[slot]))
        return out

    def start_chunk(g, slot):
        for cp in chunk_copies(g, slot):
            cp.start()

    def wait_chunk(g, slot):
        for cp in chunk_copies(g, slot):
            cp.wait()

    @pl.when(b == 0)
    def _():
        start_chunk(0, 0)
        start_chunk(1, 1)

    qa32 = qa.astype(F32)
    qr32 = qr.astype(F32)
    n_new = cn_ref.shape[0]
    tok = lax.broadcasted_iota(jnp.int32, (rows, 1), 0) // n_heads
    scores = []
    for j in range(n_new):
        sj = (jnp.sum(qa32 * cn_ref[j:j + 1, :], axis=-1, keepdims=True)
              + jnp.sum(qr32 * kn_ref[j:j + 1, :], axis=-1, keepdims=True))
        scores.append(jnp.where(tok >= j, sj, -jnp.inf))
    m = scores[0]
    for sj in scores[1:]:
        m = jnp.maximum(m, sj)
    l = jnp.zeros_like(m)
    acc = jnp.zeros((rows, kc_s.shape[1]), F32)
    for j, sj in enumerate(scores):
        pj = jnp.exp2(sj - m)
        l = l + pj
        acc = acc + pj * cn_ref[j:j + 1, :]

    def pair(cc, carry):
        for slot in range(2):
            g = b * n_chunks + 2 * cc + slot
            wait_chunk(g, slot)
            kc_s[...] = kc_buf[slot].astype(kc_s.dtype)
            for i in range(PAGES_PER_CHUNK):
                kr_s[:, i * PAGE_SIZE:(i + 1) * PAGE_SIZE] = kr_buf[slot, i].astype(kr_s.dtype)

            @pl.when(g + 2 < total)
            def _():
                start_chunk(g + 2, slot)

            kc = kc_s[...]
            s = _dot_t(qa, kc) + _dot(qr, kr_s[...])
            carry = _softmax_step(s, kc, *carry)
        return carry

    m, l, acc = lax.fori_loop(0, n_chunks // 2, pair, (m, l, acc))
    o_ref[...] = (acc / l).astype(o_ref.dtype)


def _sample_attn(page_table, qabs, qcat_s, ckv_s, kr_s, pool_c, pool_r_t, n_heads):
    nb, n_pages = page_table.shape
    kl = pool_c.shape[-1]
    rope_dim = pool_r_t.shape[-2]
    ts = qabs.shape[0]
    n_new = ts // nb
    rows = n_new * n_heads
    npc = PAGES_PER_CHUNK
    dq = qcat_s.shape[1] // n_heads
    qa3 = qabs.reshape(nb, rows, kl)
    qc3 = qcat_s.reshape(nb, rows, dq)
    cn3 = ckv_s.reshape(nb, n_new, kl)
    kn3 = kr_s.reshape(nb, n_new, rope_dim)
    pt = page_table.reshape(-1)
    assert n_pages % (2 * npc) == 0

    grid_spec = pltpu.PrefetchScalarGridSpec(
        num_scalar_prefetch=1,
        grid=(nb,),
        in_specs=[pl.BlockSpec((None, rows, kl), lambda b, pt_ref: (b, 0, 0)),
                  pl.BlockSpec((None, rows, LANES), lambda b, pt_ref: (b, 0, 1)),
                  pl.BlockSpec((None, n_new, kl), lambda b, pt_ref: (b, 0, 0)),
                  pl.BlockSpec((None, n_new, rope_dim), lambda b, pt_ref: (b, 0, 0)),
                  pl.BlockSpec(memory_space=pl.ANY), pl.BlockSpec(memory_space=pl.ANY)],
        out_specs=pl.BlockSpec((None, rows, kl), lambda b, pt_ref: (b, 0, 0)),
        scratch_shapes=[pltpu.VMEM((2, npc * PAGE_SIZE, kl), F32), pltpu.VMEM((2, npc, rope_dim, PAGE_SIZE), F32),
                        pltpu.VMEM((npc * PAGE_SIZE, kl), BF16), pltpu.VMEM((rope_dim, npc * PAGE_SIZE), BF16),
                        pltpu.SemaphoreType.DMA((2,)), pltpu.SemaphoreType.DMA((2,))],
    )
    return pl.pallas_call(
        functools.partial(_sample_attn_kernel, n_heads=n_heads, rope_dim=rope_dim, n_pages=n_pages),
        grid_spec=grid_spec,
        out_shape=jax.ShapeDtypeStruct((nb, rows, kl), BF16),
        compiler_params=_params(("arbitrary",), 40),
        name="sample_attention",
    )(pt, qa3, qc3, cn3, kn3, pool_c, pool_r_t)


def _uv(olat, wuv_h):
    n_heads, kl, dv = wuv_h.shape
    ts = olat.shape[0]
    return pl.pallas_call(
        _head_dot_kernel,
        grid=(n_heads,),
        in_specs=[pl.BlockSpec((ts, kl), lambda h: (0, h)),
                  pl.BlockSpec((None, kl, dv), lambda h: (h, 0, 0))],
        out_specs=pl.BlockSpec((ts, dv), lambda h: (0, h)),
        out_shape=jax.ShapeDtypeStruct((ts, n_heads * dv), BF16),
        compiler_params=_params(("parallel",), 32),
        name="value_up",
    )(olat, wuv_h)


def _out_kernel(x_ref, z_ref, ya_ref, g_ref, wc_ref, wo_ref, gn_ref, xo_ref, ho_ref):
    d = x_ref.shape[1]
    yc = _dot(z_ref[...], wc_ref[...])
    mix = g_ref[:, :d].astype(F32) * yc + g_ref[:, d:].astype(F32) * ya_ref[...].astype(F32)
    x1 = x_ref[...] + _dot(mix.astype(BF16), wo_ref[...])
    xo_ref[...] = x1
    ho_ref[...] = _rms(x1, gn_ref[...]).astype(ho_ref.dtype)


def _out_proj(x, z, y_attn, gates, w_conv_out, w_out, g_ffn):
    t, d = x.shape
    ch = z.shape[1]
    return pl.pallas_call(
        _out_kernel,
        grid=(t // TM_OUT,),
        in_specs=[_row_spec(TM_OUT, d), _row_spec(TM_OUT, ch), _row_spec(TM_OUT, d), _row_spec(TM_OUT, 2 * d),
                  _full_spec((ch, d)), _full_spec((d, d)), _full_spec((1, d))],
        out_specs=[_row_spec(TM_OUT, d), _row_spec(TM_OUT, d)],
        out_shape=[jax.ShapeDtypeStruct((t, d), F32), jax.ShapeDtypeStruct((t, d), BF16)],
        compiler_params=_params(("parallel",), 56),
        name="out_proj",
    )(x, z, y_attn, gates, w_conv_out, w_out, g_ffn)


def _ffn_kernel(h_ref, x_ref, wg_ref, wu_ref, wd_ref, gn_ref, xo_ref, ho_ref, acc_ref):
    f = pl.program_id(1)

    @pl.when(f == 0)
    def _():
        acc_ref[...] = x_ref[...]

    h = h_ref[...]
    g = _dot(h, wg_ref[...])
    a = (g * jax.nn.sigmoid(g) * _dot(h, wu_ref[...])).astype(BF16)
    acc_ref[...] += _dot(a, wd_ref[...])

    @pl.when(f == pl.num_programs(1) - 1)
    def _():
        x2 = acc_ref[...]
        xo_ref[...] = x2
        ho_ref[...] = _rms(x2, gn_ref[...]).astype(ho_ref.dtype)


def _ffn(h2, x1, wg, wu, wd, g_ple):
    t, d = x1.shape
    dff = wg.shape[1]
    return pl.pallas_call(
        _ffn_kernel,
        grid=(t // TM, dff // TF),
        in_specs=[pl.BlockSpec((TM, d), lambda i, f: (i, 0)), pl.BlockSpec((TM, d), lambda i, f: (i, 0)),
                  pl.BlockSpec((d, TF), lambda i, f: (0, f)), pl.BlockSpec((d, TF), lambda i, f: (0, f)),
                  pl.BlockSpec((TF, d), lambda i, f: (f, 0)), pl.BlockSpec((1, d), lambda i, f: (0, 0))],
        out_specs=[pl.BlockSpec((TM, d), lambda i, f: (i, 0)), pl.BlockSpec((TM, d), lambda i, f: (i, 0))],
        out_shape=[jax.ShapeDtypeStruct((t, d), F32), jax.ShapeDtypeStruct((t, d), BF16)],
        scratch_shapes=[pltpu.VMEM((TM, d), F32)],
        compiler_params=_params(("parallel", "arbitrary"), 56),
        name="swiglu",
    )(h2, x1, wg, wu, wd, g_ple)


def _ple_kernel(h_ref, x_ref, p_ref, wg_ref, wp_ref, gf_ref, y_ref):
    gate = jax.nn.sigmoid(_dot(h_ref[...], wg_ref[...]))
    x3 = x_ref[...] + gate * _dot(p_ref[...].astype(BF16), wp_ref[...])
    y_ref[...] = _rms(x3, gf_ref[...])


def _ple_final(h3, x2, pemb, w_gate, w_ple, g_final):
    n, pd = pemb.shape
    d = x2.shape[1]
    return pl.pallas_call(
        _ple_kernel,
        grid=(n // TM,),
        in_specs=[_row_spec(TM, d), _row_spec(TM, d),
                  _row_spec(TM, pd), _full_spec((d, d)), _full_spec((pd, d)), _full_spec((1, d))],
        out_specs=_row_spec(TM, d),
        out_shape=jax.ShapeDtypeStruct((n, d), F32),
        compiler_params=_params(("parallel",), 48),
        name="ple_final",
    )(h3, x2, pemb, w_gate, w_ple, g_final)


def _rope_tables(pos, half):
    inv = ROPE_THETA ** (-jnp.arange(half, dtype=F32) / half)
    ang = pos[:, None] * inv[None, :]
    cos = jnp.cos(ang)
    sin = jnp.sin(ang)
    pad = jnp.zeros((pos.shape[0], LANES - 2 * half), F32)
    return jnp.concatenate([cos, cos, pad], axis=-1), jnp.concatenate([sin, sin, pad], axis=-1)


def _rot_cols(w, half):
    return jnp.concatenate([-w[..., half:], w[..., :half]], axis=-1)


def _pad_lanes(w):
    return jnp.concatenate([w, jnp.zeros(w.shape[:-1] + (LANES - w.shape[-1],), w.dtype)], axis=-1)


def kernel(x_prompt, x_sample, p_prompt, p_sample, cache_ckv, cache_krope, page_table, state_conv, g_mix_norm, w_in, b_gate, g_q, w_uq, g_kv, w_ukv, w_dw, b_dw, g_conv_ln, b_conv_ln, w_conv_out, w_out, g_ffn_norm, w_ffn_gate, w_ffn_up, w_ffn_down, w_ple, g_ple_norm, w_ple_gate, g_final):
    depth = w_in.shape[0]
    assert depth == 1, "single-layer step only"
    batch, seq, d = x_prompt.shape
    nb, n_new, _ = x_sample.shape
    t_prompt = batch * seq
    t_sample = nb * n_new
    ch = w_dw.shape[2]
    n_state = state_conv.shape[2]
    q_lora = g_q.shape[1]
    kv_lora = g_kv.shape[1]
    n_heads = w_uq.shape[2]
    rope_dim = cache_krope.shape[-1]
    nope_dim = w_uq.shape[3] - rope_dim
    half = rope_dim // 2
    qscale = LOG2E / math.sqrt(nope_dim + rope_dim)
    assert batch == 1 and seq >= n_state
    assert nope_dim == LANES and w_ukv.shape[3] - nope_dim == LANES and 2 * half == rope_dim <= LANES
    assert t_prompt % TM == 0 and t_sample % TM == 0 and t_prompt % TM_CONV == 0 and t_prompt % TQ == 0
    assert nb % SEQ_PER_STEP == 0 and cache_ckv.shape[2] == PAGE_SIZE

    off_glu = 2 * ch
    off_q = off_glu + q_lora
    off_kv = off_q + kv_lora
    off_kr = off_kv + rope_dim

    wi = w_in[0]
    wa = wi[:, :ch].astype(BF16)
    wb = wi[:, ch:off_glu].astype(BF16)
    wq = wi[:, off_glu:off_q].astype(BF16)
    wkv = wi[:, off_q:off_kv].astype(BF16)
    wkr = wi[:, off_kv:off_kr]
    wkr_p = _pad_lanes(wkr).astype(BF16)
    wkt_p = _pad_lanes(_rot_cols(wkr, half)).astype(BF16)
    wgt = wi[:, off_kr:].astype(BF16)
    wq_n = w_uq[0][:, :, :nope_dim].astype(BF16)
    wq_rope = w_uq[0][:, :, nope_dim:]
    wq_r = _pad_lanes(wq_rope).astype(BF16)
    wq_t = _pad_lanes(_rot_cols(wq_rope, half)).astype(BF16)
    flat = lambda w: w.reshape(q_lora, n_heads * LANES)
    flat_t = lambda w: jnp.transpose(w, (1, 2, 0)).reshape(n_heads * LANES, q_lora)
    wuk = w_ukv[0][:, :, :nope_dim]
    wuv = w_ukv[0][:, :, nope_dim:]
    wuk_flat = wuk.reshape(kv_lora, n_heads * nope_dim).astype(BF16)
    wuv_flat_t = jnp.transpose(wuv, (1, 2, 0)).reshape(n_heads * LANES, kv_lora).astype(BF16)
    wuk_t = jnp.transpose(wuk, (1, 2, 0)).astype(BF16)
    wuv_h = jnp.transpose(wuv, (1, 0, 2)).astype(BF16)
    row = lambda a: a.reshape(1, -1)

    n_past = page_table.shape[1] * PAGE_SIZE
    cs_p, sn_p = _rope_tables(jnp.arange(seq, dtype=F32), half)
    cs_s, sn_s = _rope_tables(jnp.tile(n_past + jnp.arange(n_new, dtype=F32), nb), half)

    xp = x_prompt.reshape(t_prompt, d)
    xs = x_sample.reshape(t_sample, d)

    g_mix = row(g_mix_norm[0])
    hp = _norm(xp, g_mix)
    hs = _norm(xs, g_mix)
    up = _glu(hp, wa, wb)
    us = _glu(hs, wa, wb)
    gq = row(g_q[0])
    q_t = _q_cols(hp, wq, gq, flat_t(wq_n), flat_t(wq_r), flat_t(wq_t), (cs_p * qscale).T, (sn_p * qscale).T,
                  n_heads, qscale)
    qcat_s = _q_rows(hs, wq, gq, flat(wq_n), flat(wq_r), flat(wq_t), cs_s * qscale, sn_s * qscale, n_heads, qscale)
    gkv = row(g_kv[0])
    ckv_p, kr_p, kcat, v_t = _kv_proj(hp, wkv, gkv, wkr_p, wkt_p, cs_p, sn_p, n_heads, rope_dim, wuk_flat, wuv_flat_t)
    ckv_s, kr_s = _kv_proj(hs, wkv, gkv, wkr_p, wkt_p, cs_s, sn_s, n_heads, rope_dim)
    bg = row(b_gate[0])
    gates_p = _gates(hp, wgt, bg)
    gates_s = _gates(hs, wgt, bg)

    conv_args = (w_dw[0], row(b_dw[0]), row(g_conv_ln[0]), row(b_conv_ln[0]))
    zp = _conv_prompt(up, *conv_args)
    state_t = jnp.transpose(state_conv[0], (1, 0, 2))
    u_t = jnp.transpose(us.reshape(nb, n_new, ch), (1, 0, 2))
    zs = jnp.transpose(_conv_sample(state_t, u_t, *conv_args), (1, 0, 2)).reshape(t_sample, ch)

    ya_p = _flash(q_t, kcat, v_t, n_heads)
    qabs = _absorb(qcat_s, wuk_t)
    olat = _sample_attn(page_table, qabs, qcat_s, ckv_s, kr_s, cache_ckv, jnp.swapaxes(cache_krope, 2, 3), n_heads)
    ya_s = _uv(olat.reshape(t_sample, n_heads * kv_lora), wuv_h)

    out_w = (w_conv_out[0].astype(BF16), w_out[0].astype(BF16), row(g_ffn_norm[0]))
    ffn_w = (w_ffn_gate[0].astype(BF16), w_ffn_up[0].astype(BF16), w_ffn_down[0].astype(BF16), row(g_ple_norm[0]))
    ple_w = (w_ple_gate[0].astype(BF16), w_ple[0].astype(BF16), row(g_final))
    outs = []
    for x, z, ya, gates, pemb in ((xp, zp, ya_p, gates_p, p_prompt[0].reshape(t_prompt, -1)),
                                  (xs, zs, ya_s, gates_s, p_sample[0].reshape(t_sample, -1))):
        x1, h2 = _out_proj(x, z, ya, gates, *out_w)
        x2, h3 = _ffn(h2, x1, *ffn_w)
        outs.append(_ple_final(h3, x2, pemb, *ple_w))
    y_prompt, y_sample = outs

    conv_p = up[t_prompt - n_state:].reshape(1, batch, n_state, ch)
    conv_s = jnp.transpose(jnp.concatenate([state_t[n_new:], u_t], axis=0), (1, 0, 2))[None]
    return (y_prompt.reshape(batch, seq, d), y_sample.reshape(nb, n_new, d),
            ckv_p.reshape(1, batch, seq, kv_lora), kr_p.reshape(1, batch, seq, rope_dim), conv_p,
            ckv_s.reshape(1, nb, n_new, kv_lora), kr_s.reshape(1, nb, n_new, rope_dim), conv_s)
```

```python
import functools
import math

import jax
import jax.numpy as jnp
from jax import lax
from jax.experimental import pallas as pl
from jax.experimental.pallas import tpu as pltpu

F32 = jnp.float32
BF16 = jnp.bfloat16

EPS = 1e-6
ROPE_THETA = 10000.0
PAGE_SIZE = 128
LANES = 128
BF16_ROWS = 16
MIB = 1024 * 1024
LOG2E = math.log2(math.e)

TM = 512
TM_OUT = 256
TM_CONV = 256
CONV_ROWS = 64
CONV_HALO = 32
SEQ_PER_STEP = 32
TQ = 1024
TK = 1024
FLASH_HEADS = 2
PAGES_PER_CHUNK = 16
TF = 512


def _params(semantics, vmem_mib):
    return pltpu.CompilerParams(dimension_semantics=semantics, vmem_limit_bytes=vmem_mib * MIB)


def _dot(a, b):
    return jnp.dot(a, b, preferred_element_type=F32)


def _dot_t(a, b):
    return lax.dot_general(a, b, (((1,), (1,)), ((), ())), preferred_element_type=F32)


def _rms(x, g):
    return x * lax.rsqrt(jnp.mean(x * x, axis=-1, keepdims=True) + EPS) * g


def _row_spec(tm, n):
    return pl.BlockSpec((tm, n), lambda i: (i, 0))


def _col_spec(n, tm):
    return pl.BlockSpec((n, tm), lambda i: (0, i))


def _full_spec(shape):
    return pl.BlockSpec(shape, lambda *_: (0,) * len(shape))


def _norm_kernel(x_ref, g_ref, o_ref):
    o_ref[...] = _rms(x_ref[...], g_ref[...]).astype(o_ref.dtype)


def _norm(x, g):
    t, d = x.shape
    return pl.pallas_call(
        _norm_kernel,
        grid=(t // TM,),
        in_specs=[_row_spec(TM, d), _full_spec((1, d))],
        out_specs=_row_spec(TM, d),
        out_shape=jax.ShapeDtypeStruct((t, d), BF16),
        compiler_params=_params(("parallel",), 32),
        name="mix_norm",
    )(x, g)


def _glu_kernel(h_ref, wa_ref, wb_ref, o_ref):
    h = h_ref[...]
    o_ref[...] = _dot(h, wa_ref[...]) * jax.nn.sigmoid(_dot(h, wb_ref[...]))


def _glu(h, wa, wb):
    t, d = h.shape
    c = wa.shape[1]
    return pl.pallas_call(
        _glu_kernel,
        grid=(t // TM,),
        in_specs=[_row_spec(TM, d), _full_spec((d, c)), _full_spec((d, c))],
        out_specs=_row_spec(TM, c),
        out_shape=jax.ShapeDtypeStruct((t, c), F32),
        compiler_params=_params(("parallel",), 40),
        name="glu",
    )(h, wa, wb)


def _q_rows_kernel(h_ref, wq_ref, gq_ref, wn_ref, wr_ref, wt_ref, cs_ref, sn_ref, o_ref, *, n_heads, qscale):
    ql = _rms(_dot(h_ref[...], wq_ref[...]), gq_ref[...]).astype(BF16)
    qn = _dot(ql, wn_ref[...]) * qscale
    qr = _dot(ql, wr_ref[...])
    qt = _dot(ql, wt_ref[...])
    cs = cs_ref[...]
    sn = sn_ref[...]
    for h in range(n_heads):
        lo = h * LANES
        o_ref[:, 2 * lo:2 * lo + LANES] = qn[:, lo:lo + LANES].astype(o_ref.dtype)
        rot = qr[:, lo:lo + LANES] * cs + qt[:, lo:lo + LANES] * sn
        o_ref[:, 2 * lo + LANES:2 * lo + 2 * LANES] = rot.astype(o_ref.dtype)


def _q_rows(h, wq, gq, wn, wr, wt, cs, sn, n_heads, qscale):
    n_rows, d = h.shape
    ql = wq.shape[1]
    hw = wn.shape[1]
    return pl.pallas_call(
        functools.partial(_q_rows_kernel, n_heads=n_heads, qscale=qscale),
        grid=(n_rows // TM,),
        in_specs=[_row_spec(TM, d), _full_spec((d, ql)), _full_spec((1, ql)), _full_spec((ql, hw)),
                  _full_spec((ql, hw)), _full_spec((ql, hw)), _row_spec(TM, LANES), _row_spec(TM, LANES)],
        out_specs=_row_spec(TM, 2 * hw),
        out_shape=jax.ShapeDtypeStruct((n_rows, 2 * hw), BF16),
        compiler_params=_params(("parallel",), 56),
        name="q_rows",
    )(h, wq, gq, wn, wr, wt, cs, sn)


def _q_cols_kernel(h_ref, wq_ref, gq_ref, wn_ref, wr_ref, wt_ref, cs_ref, sn_ref, o_ref, *, n_heads, qscale):
    ql = _rms(_dot(h_ref[...], wq_ref[...]), gq_ref[...])
    ql_t = ql.T.astype(BF16)
    qn = _dot(wn_ref[...], ql_t) * qscale
    qr = _dot(wr_ref[...], ql_t)
    qt = _dot(wt_ref[...], ql_t)
    cs = cs_ref[...]
    sn = sn_ref[...]
    for h in range(n_heads):
        lo = h * LANES
        o_ref[2 * lo:2 * lo + LANES, :] = qn[lo:lo + LANES, :].astype(o_ref.dtype)
        rot = qr[lo:lo + LANES, :] * cs + qt[lo:lo + LANES, :] * sn
        o_ref[2 * lo + LANES:2 * lo + 2 * LANES, :] = rot.astype(o_ref.dtype)


def _q_cols(h, wq, gq, wn_t, wr_t, wt_t, cs_t, sn_t, n_heads, qscale):
    n_rows, d = h.shape
    ql = wq.shape[1]
    hw = wn_t.shape[0]
    return pl.pallas_call(
        functools.partial(_q_cols_kernel, n_heads=n_heads, qscale=qscale),
        grid=(n_rows // TM,),
        in_specs=[_row_spec(TM, d), _full_spec((d, ql)), _full_spec((1, ql)), _full_spec((hw, ql)),
                  _full_spec((hw, ql)), _full_spec((hw, ql)), _col_spec(LANES, TM), _col_spec(LANES, TM)],
        out_specs=_col_spec(2 * hw, TM),
        out_shape=jax.ShapeDtypeStruct((2 * hw, n_rows), BF16),
        compiler_params=_params(("parallel",), 56),
        name="q_cols",
    )(h, wq, gq, wn_t, wr_t, wt_t, cs_t, sn_t)


def _kv_kernel(h_ref, wkv_ref, gkv_ref, wkr_ref, wkt_ref, cs_ref, sn_ref, *rest, n_heads, rope_dim, v_rows, prompt):
    if prompt:
        wuk_ref, wuv_ref, ckv_ref, kr_ref, kcat_ref, vt_ref = rest
    else:
        ckv_ref, kr_ref = rest
    h = h_ref[...]
    ckv = _rms(_dot(h, wkv_ref[...]), gkv_ref[...])
    ckv_ref[...] = ckv
    kr = _dot(h, wkr_ref[...]) * cs_ref[...] + _dot(h, wkt_ref[...]) * sn_ref[...]
    kr_ref[...] = kr[:, :rope_dim]
    if not prompt:
        return
    kr_b = kr.astype(kcat_ref.dtype)
    kn = _dot(ckv.astype(BF16), wuk_ref[...]).astype(kcat_ref.dtype)
    vt = _dot(wuv_ref[...], ckv.T.astype(BF16)).astype(vt_ref.dtype)
    tm = h.shape[0]
    ones_rows = jnp.where(lax.broadcasted_iota(jnp.int32, (v_rows - LANES, tm), 0) == 0, 1.0, 0.0).astype(vt_ref.dtype)
    for hd in range(n_heads):
        lo = hd * LANES
        kcat_ref[:, 2 * lo:2 * lo + LANES] = kn[:, lo:lo + LANES]
        kcat_ref[:, 2 * lo + LANES:2 * lo + 2 * LANES] = kr_b
        vt_ref[hd * v_rows:hd * v_rows + LANES, :] = vt[lo:lo + LANES, :]
        vt_ref[hd * v_rows + LANES:(hd + 1) * v_rows, :] = ones_rows


def _kv_proj(h, wkv, gkv, wkr, wkt, cs, sn, n_heads, rope_dim, wuk=None, wuv_t=None):
    n_rows, d = h.shape
    kl = wkv.shape[1]
    prompt = wuk is not None
    v_rows = LANES + BF16_ROWS
    in_specs = [_row_spec(TM, d), _full_spec((d, kl)), _full_spec((1, kl)), _full_spec((d, LANES)),
                _full_spec((d, LANES)), _row_spec(TM, LANES), _row_spec(TM, LANES)]
    out_specs = [_row_spec(TM, kl), _row_spec(TM, rope_dim)]
    out_shape = [jax.ShapeDtypeStruct((n_rows, kl), F32), jax.ShapeDtypeStruct((n_rows, rope_dim), F32)]
    args = [h, wkv, gkv, wkr, wkt, cs, sn]
    if prompt:
        hw = wuk.shape[1]
        in_specs += [_full_spec((kl, hw)), _full_spec((hw, kl))]
        out_specs += [_row_spec(TM, 2 * hw), _col_spec(n_heads * v_rows, TM)]
        out_shape += [jax.ShapeDtypeStruct((n_rows, 2 * hw), BF16),
                      jax.ShapeDtypeStruct((n_heads * v_rows, n_rows), BF16)]
        args += [wuk, wuv_t]
    return pl.pallas_call(
        functools.partial(_kv_kernel, n_heads=n_heads, rope_dim=rope_dim, v_rows=v_rows, prompt=prompt),
        grid=(n_rows // TM,),
        in_specs=in_specs,
        out_specs=out_specs,
        out_shape=out_shape,
        compiler_params=_params(("parallel",), 56),
        name="kv_proj_prompt" if prompt else "kv_proj_sample",
    )(*args)


def _gate_kernel(h_ref, w_ref, b_ref, o_ref):
    o_ref[...] = jax.nn.sigmoid(_dot(h_ref[...], w_ref[...]) + b_ref[...]).astype(o_ref.dtype)


def _gates(h, w, b, tn=1024):
    t, d = h.shape
    n = w.shape[1]
    return pl.pallas_call(
        _gate_kernel,
        grid=(n // tn, t // TM),
        in_specs=[pl.BlockSpec((TM, d), lambda j, i: (i, 0)), pl.BlockSpec((d, tn), lambda j, i: (0, j)),
                  pl.BlockSpec((1, tn), lambda j, i: (0, j))],
        out_specs=pl.BlockSpec((TM, tn), lambda j, i: (i, j)),
        out_shape=jax.ShapeDtypeStruct((t, n), BF16),
        compiler_params=_params(("parallel", "parallel"), 40),
        name="branch_gates",
    )(h, w, b)


def _ln_silu(y, g, b):
    mu = jnp.mean(y, axis=-1, keepdims=True)
    d = y - mu
    var = jnp.mean(d * d, axis=-1, keepdims=True)
    z = d * lax.rsqrt(var + EPS) * g + b
    return z * jax.nn.sigmoid(z)


def _conv_prompt_kernel(prev_ref, cur_ref, w_ref, bdw_ref, g_ref, b_ref, o_ref, ext_ref, y_ref, *, width):
    i = pl.program_id(0)
    tm, ch = cur_ref.shape

    @pl.when(i == 0)
    def _():
        ext_ref[0:CONV_HALO, :] = jnp.zeros((CONV_HALO, ch), F32)

    @pl.when(i > 0)
    def _():
        ext_ref[0:CONV_HALO, :] = prev_ref[...]

    ext_ref[CONV_HALO:CONV_HALO + tm, :] = cur_ref[...]

    lead = CONV_HALO - (width - 1)
    by_shift = {}
    for k in range(width):
        a, s = divmod(lead + k, 8)
        by_shift.setdefault(s, []).append((a, k))
    for r0 in range(0, tm, CONV_ROWS):
        for c0 in range(0, ch, LANES):
            acc = jnp.zeros((CONV_ROWS, LANES), F32) + bdw_ref[:, c0:c0 + LANES]
            for s, taps in by_shift.items():
                a_lo = min(a for a, _ in taps)
                a_hi = max(a for a, _ in taps)
                win = ext_ref[r0 + s + 8 * a_lo:r0 + s + 8 * a_hi + CONV_ROWS, c0:c0 + LANES]
                for a, k in taps:
                    off = 8 * (a - a_lo)
                    acc = acc + win[off:off + CONV_ROWS, :] * w_ref[k:k + 1, c0:c0 + LANES]
            y_ref[r0:r0 + CONV_ROWS, c0:c0 + LANES] = acc
    o_ref[...] = _ln_silu(y_ref[...], g_ref[...], b_ref[...]).astype(o_ref.dtype)


def _conv_prompt(u, w_dw, b_dw, g_ln, b_ln):
    t, ch = u.shape
    width = w_dw.shape[0]
    per = TM_CONV // CONV_HALO
    return pl.pallas_call(
        functools.partial(_conv_prompt_kernel, width=width),
        grid=(t // TM_CONV,),
        in_specs=[pl.BlockSpec((CONV_HALO, ch), lambda i: (jnp.maximum(i * per - 1, 0), 0)),
                  _row_spec(TM_CONV, ch), _full_spec((width, ch)), _full_spec((1, ch)),
                  _full_spec((1, ch)), _full_spec((1, ch))],
        out_specs=_row_spec(TM_CONV, ch),
        out_shape=jax.ShapeDtypeStruct((t, ch), BF16),
        scratch_shapes=[pltpu.VMEM((TM_CONV + CONV_HALO, ch), F32), pltpu.VMEM((TM_CONV, ch), F32)],
        compiler_params=_params(("arbitrary",), 32),
        name="conv_prompt",
    )(u, u, w_dw, b_dw, g_ln, b_ln)


def _conv_sample_kernel(st_ref, u_ref, w_ref, bdw_ref, g_ref, b_ref, o_ref):
    n_state = st_ref.shape[0]
    n_new = u_ref.shape[0]
    width = w_ref.shape[0]
    for t in range(n_new):
        acc = jnp.zeros(o_ref.shape[1:], F32) + bdw_ref[...]
        for k in range(width):
            j = t + k
            src = st_ref[j] if j < n_state else u_ref[j - n_state]
            acc = acc + src * w_ref[k:k + 1, :]
        o_ref[t] = _ln_silu(acc, g_ref[...], b_ref[...]).astype(o_ref.dtype)


def _conv_sample(state_t, u_t, w_dw, b_dw, g_ln, b_ln):
    n_state, nb, ch = state_t.shape
    n_new = u_t.shape[0]
    width = w_dw.shape[0]
    assert width == n_state + 1
    return pl.pallas_call(
        _conv_sample_kernel,
        grid=(nb // SEQ_PER_STEP,),
        in_specs=[pl.BlockSpec((n_state, SEQ_PER_STEP, ch), lambda i: (0, i, 0)),
                  pl.BlockSpec((n_new, SEQ_PER_STEP, ch), lambda i: (0, i, 0)),
                  _full_spec((width, ch)), _full_spec((1, ch)), _full_spec((1, ch)), _full_spec((1, ch))],
        out_specs=pl.BlockSpec((n_new, SEQ_PER_STEP, ch), lambda i: (0, i, 0)),
        out_shape=jax.ShapeDtypeStruct((n_new, nb, ch), BF16),
        compiler_params=_params(("parallel",), 32),
        name="conv_sample",
    )(state_t, u_t, w_dw, b_dw, g_ln, b_ln)


def _flash_kernel(qt_ref, k_ref, vt_ref, o_ref):
    qi = pl.program_id(1)
    tq = qt_ref.shape[1]
    dq = qt_ref.shape[0] // FLASH_HEADS
    v_rows = vt_ref.shape[0] // FLASH_HEADS
    dv = o_ref.shape[1] // FLASH_HEADS

    def block(off, carry, first_key):
        out = []
        for hd in range(FLASH_HEADS):
            m, acc = carry[hd]
            s = _dot(k_ref[pl.ds(off, TK), hd * dq:(hd + 1) * dq], qt_ref[hd * dq:(hd + 1) * dq, :])
            if first_key is not None:
                key = first_key + lax.broadcasted_iota(jnp.int32, (TK, tq), 0)
                qry = lax.broadcasted_iota(jnp.int32, (TK, tq), 1)
                s = jnp.where(key <= qry, s, -jnp.inf)
            m_new = jnp.maximum(m, jnp.max(s, axis=0, keepdims=True))
            p = jnp.exp2(s - m_new).astype(vt_ref.dtype)
            pv = _dot(vt_ref[hd * v_rows:(hd + 1) * v_rows, pl.ds(off, TK)], p)
            out.append((m_new, jnp.exp2(m - m_new) * acc + pv))
        return tuple(out)

    per = tq // TK
    init = tuple((jnp.full((1, tq), -jnp.inf, F32), jnp.zeros((v_rows, tq), F32)) for _ in range(FLASH_HEADS))
    carry = lax.fori_loop(0, qi * per, lambda kb, c: block(pl.multiple_of(kb * TK, TK), c, None), init)
    for j in range(per):
        carry = block(pl.multiple_of(qi * tq + j * TK, TK), carry, j * TK)
    for hd in range(FLASH_HEADS):
        acc = carry[hd][1]
        o_ref[:, hd * dv:(hd + 1) * dv] = (acc[:dv, :] / acc[dv:dv + 1, :]).T.astype(o_ref.dtype)


def _flash(q_t, kcat, v_t, n_heads):
    t_prompt = q_t.shape[1]
    dq = q_t.shape[0] // n_heads
    v_rows = v_t.shape[0] // n_heads
    nh = FLASH_HEADS
    assert n_heads % nh == 0
    return pl.pallas_call(
        _flash_kernel,
        grid=(n_heads // nh, t_prompt // TQ),
        in_specs=[pl.BlockSpec((nh * dq, TQ), lambda h, i: (h, i)),
                  pl.BlockSpec((t_prompt, nh * dq), lambda h, i: (0, h)),
                  pl.BlockSpec((nh * v_rows, t_prompt), lambda h, i: (h, 0))],
        out_specs=pl.BlockSpec((TQ, nh * LANES), lambda h, i: (i, h)),
        out_shape=jax.ShapeDtypeStruct((t_prompt, n_heads * LANES), BF16),
        compiler_params=_params(("parallel", "arbitrary"), 56),
        name="prompt_attention",
    )(q_t, kcat, v_t)


def _head_dot_kernel(q_ref, w_ref, o_ref):
    o_ref[...] = _dot(q_ref[...], w_ref[...]).astype(o_ref.dtype)


def _absorb(qcat_s, wuk_t):
    n_heads, dn, kl = wuk_t.shape
    ts = qcat_s.shape[0]
    return pl.pallas_call(
        _head_dot_kernel,
        grid=(n_heads,),
        in_specs=[pl.BlockSpec((ts, dn), lambda h: (0, 2 * h)),
                  pl.BlockSpec((None, dn, kl), lambda h: (h, 0, 0))],
        out_specs=pl.BlockSpec((ts, kl), lambda h: (0, h)),
        out_shape=jax.ShapeDtypeStruct((ts, n_heads * kl), BF16),
        compiler_params=_params(("parallel",), 32),
        name="absorb_q",
    )(qcat_s, wuk_t)


def _softmax_step(s, v, m, l, acc):
    m_new = jnp.maximum(m, jnp.max(s, axis=-1, keepdims=True))
    alpha = jnp.exp2(m - m_new)
    p = jnp.exp2(s - m_new)
    l_new = alpha * l + jnp.sum(p, axis=-1, keepdims=True)
    acc_new = alpha * acc + _dot(p.astype(v.dtype), v)
    return m_new, l_new, acc_new


def _sample_attn_kernel(pt_ref, qa_ref, qr_ref, cn_ref, kn_ref, poolc_ref, poolr_ref, o_ref,
                        kc_buf, kr_buf, kc_s, kr_s, sem_c, sem_r, *, n_heads, rope_dim, n_pages):
    b = pl.program_id(0)
    n_chunks = n_pages // PAGES_PER_CHUNK
    total = pl.num_programs(0) * n_chunks
    qa = qa_ref[...]
    qr = qr_ref[:, :rope_dim]
    rows = qa.shape[0]

    def chunk_copies(g, slot):
        base = g * PAGES_PER_CHUNK
        out = []
        for i in range(PAGES_PER_CHUNK):
            page = pt_ref[base + i]
            out.append(pltpu.make_async_copy(poolc_ref.at[0, page],
                                             kc_buf.at[slot, pl.ds(i * PAGE_SIZE, PAGE_SIZE)], sem_c.at[slot]))
            out.append(pltpu.make_async_copy(poolr_ref.at[0, page], kr_buf.at[slot, i], sem_r.at[slot]))
        return out

    def start_chunk(g, slot):
        for n, cp in enumerate(chunk_copies(g, slot)):
            cp.start(priority=(n // 2 + n) % 2)

    def wait_chunk(g, slot):
        for cp in chunk_copies(g, slot):
            cp.wait()

    @pl.when(b == 0)
    def _():
        start_chunk(0, 0)
        start_chunk(1, 1)

    qa32 = qa.astype(F32)
    qr32 = qr.astype(F32)
    n_new = cn_ref.shape[0]
    tok = lax.broadcasted_iota(jnp.int32, (rows, 1), 0) // n_heads
    scores = []
    for j in range(n_new):
        sj = (jnp.sum(qa32 * cn_ref[j:j + 1, :], axis=-1, keepdims=True)
              + jnp.sum(qr32 * kn_ref[j:j + 1, :], axis=-1, keepdims=True))
        scores.append(jnp.where(tok >= j, sj, -jnp.inf))
    m = scores[0]
    for sj in scores[1:]:
        m = jnp.maximum(m, sj)
    l = jnp.zeros_like(m)
    acc = jnp.zeros((rows, kc_s.shape[1]), F32)
    for j, sj in enumerate(scores):
        pj = jnp.exp2(sj - m)
        l = l + pj
        acc = acc + pj * cn_ref[j:j + 1, :]

    def pair(cc, carry):
        for slot in range(2):
            g = b * n_chunks + 2 * cc + slot
            wait_chunk(g, slot)
            kc_s[...] = kc_buf[slot].astype(kc_s.dtype)
            for i in range(PAGES_PER_CHUNK):
                kr_s[:, i * PAGE_SIZE:(i + 1) * PAGE_SIZE] = kr_buf[slot, i].astype(kr_s.dtype)

            @pl.when(g + 2 < total)
            def _():
                start_chunk(g + 2, slot)

            kc = kc_s[...]
            s = _dot_t(qa, kc) + _dot(qr, kr_s[...])
            carry = _softmax_step(s, kc, *carry)
        return carry

    m, l, acc = lax.fori_loop(0, n_chunks // 2, pair, (m, l, acc))
    o_ref[...] = (acc / l).astype(o_ref.dtype)


def _sample_attn(page_table, qabs, qcat_s, ckv_s, kr_s, pool_c, pool_r_t, n_heads):
    nb, n_pages = page_table.shape
    kl = pool_c.shape[-1]
    rope_dim = pool_r_t.shape[-2]
    ts = qabs.shape[0]
    n_new = ts // nb
    rows = n_new * n_heads
    npc = PAGES_PER_CHUNK
    dq = qcat_s.shape[1] // n_heads
    qa3 = qabs.reshape(nb, rows, kl)
    qc3 = qcat_s.reshape(nb, rows, dq)
    cn3 = ckv_s.reshape(nb, n_new, kl)
    kn3 = kr_s.reshape(nb, n_new, rope_dim)
    pt = page_table.reshape(-1)
    assert n_pages % (2 * npc) == 0

    grid_spec = pltpu.PrefetchScalarGridSpec(
        num_scalar_prefetch=1,
        grid=(nb,),
        in_specs=[pl.BlockSpec((None, rows, kl), lambda b, pt_ref: (b, 0, 0)),
                  pl.BlockSpec((None, rows, LANES), lambda b, pt_ref: (b, 0, 1)),
                  pl.BlockSpec((None, n_new, kl), lambda b, pt_ref: (b, 0, 0)),
                  pl.BlockSpec((None, n_new, rope_dim), lambda b, pt_ref: (b, 0, 0)),
                  pl.BlockSpec(memory_space=pl.ANY), pl.BlockSpec(memory_space=pl.ANY)],
        out_specs=pl.BlockSpec((None, rows, kl), lambda b, pt_ref: (b, 0, 0)),
        scratch_shapes=[pltpu.VMEM((2, npc * PAGE_SIZE, kl), F32), pltpu.VMEM((2, npc, rope_dim, PAGE_SIZE), F32),
                        pltpu.VMEM((npc * PAGE_SIZE, kl), BF16), pltpu.VMEM((rope_dim, npc * PAGE_SIZE), BF16),
                        pltpu.SemaphoreType.DMA((2,)), pltpu.SemaphoreType.DMA((2,))],
    )
    return pl.pallas_call(
        functools.partial(_sample_attn_kernel, n_heads=n_heads, rope_dim=rope_dim, n_pages=n_pages),
        grid_spec=grid_spec,
        out_shape=jax.ShapeDtypeStruct((nb, rows, kl), BF16),
        compiler_params=_params(("arbitrary",), 40),
        name="sample_attention",
    )(pt, qa3, qc3, cn3, kn3, pool_c, pool_r_t)


def _uv(olat, wuv_h):
    n_heads, kl, dv = wuv_h.shape
    ts = olat.shape[0]
    return pl.pallas_call(
        _head_dot_kernel,
        grid=(n_heads,),
        in_specs=[pl.BlockSpec((ts, kl), lambda h: (0, h)),
                  pl.BlockSpec((None, kl, dv), lambda h: (h, 0, 0))],
        out_specs=pl.BlockSpec((ts, dv), lambda h: (0, h)),
        out_shape=jax.ShapeDtypeStruct((ts, n_heads * dv), BF16),
        compiler_params=_params(("parallel",), 32),
        name="value_up",
    )(olat, wuv_h)


def _out_kernel(x_ref, z_ref, ya_ref, g_ref, wc_ref, wo_ref, gn_ref, xo_ref, ho_ref):
    d = x_ref.shape[1]
    yc = _dot(z_ref[...], wc_ref[...])
    mix = g_ref[:, :d].astype(F32) * yc + g_ref[:, d:].astype(F32) * ya_ref[...].astype(F32)
    x1 = x_ref[...] + _dot(mix.astype(BF16), wo_ref[...])
    xo_ref[...] = x1
    ho_ref[...] = _rms(x1, gn_ref[...]).astype(ho_ref.dtype)


def _out_proj(x, z, y_attn, gates, w_conv_out, w_out, g_ffn):
    t, d = x.shape
    ch = z.shape[1]
    return pl.pallas_call(
        _out_kernel,
        grid=(t // TM_OUT,),
        in_specs=[_row_spec(TM_OUT, d), _row_spec(TM_OUT, ch), _row_spec(TM_OUT, d), _row_spec(TM_OUT, 2 * d),
                  _full_spec((ch, d)), _full_spec((d, d)), _full_spec((1, d))],
        out_specs=[_row_spec(TM_OUT, d), _row_spec(TM_OUT, d)],
        out_shape=[jax.ShapeDtypeStruct((t, d), F32), jax.ShapeDtypeStruct((t, d), BF16)],
        compiler_params=_params(("parallel",), 56),
        name="out_proj",
    )(x, z, y_attn, gates, w_conv_out, w_out, g_ffn)


def _ffn_kernel(h_ref, x_ref, wg_ref, wu_ref, wd_ref, gn_ref, xo_ref, ho_ref, acc_ref):
    f = pl.program_id(1)

    @pl.when(f == 0)
    def _():
        acc_ref[...] = x_ref[...]

    h = h_ref[...]
    g = _dot(h, wg_ref[...])
    a = (g * jax.nn.sigmoid(g) * _dot(h, wu_ref[...])).astype(BF16)
    acc_ref[...] += _dot(a, wd_ref[...])

    @pl.when(f == pl.num_programs(1) - 1)
    def _():
        x2 = acc_ref[...]
        xo_ref[...] = x2
        ho_ref[...] = _rms(x2, gn_ref[...]).astype(ho_ref.dtype)


def _ffn_tiles(w):
    d, dff = w.shape
    return jnp.transpose(w.reshape(d, dff // TF, TF), (1, 0, 2))


def _ffn(h2, x1, wg, wu, wd, g_ple):
    t, d = x1.shape
    nf = wg.shape[0]
    return pl.pallas_call(
        _ffn_kernel,
        grid=(t // TM, nf),
        in_specs=[pl.BlockSpec((TM, d), lambda i, f: (i, 0)), pl.BlockSpec((TM, d), lambda i, f: (i, 0)),
                  pl.BlockSpec((None, d, TF), lambda i, f: (f, 0, 0)),
                  pl.BlockSpec((None, d, TF), lambda i, f: (f, 0, 0)),
                  pl.BlockSpec((TF, d), lambda i, f: (f, 0)), pl.BlockSpec((1, d), lambda i, f: (0, 0))],
        out_specs=[pl.BlockSpec((TM, d), lambda i, f: (i, 0)), pl.BlockSpec((TM, d), lambda i, f: (i, 0))],
        out_shape=[jax.ShapeDtypeStruct((t, d), F32), jax.ShapeDtypeStruct((t, d), BF16)],
        scratch_shapes=[pltpu.VMEM((TM, d), F32)],
        compiler_params=_params(("parallel", "arbitrary"), 56),
        name="swiglu",
    )(h2, x1, wg, wu, wd, g_ple)


def _ple_kernel(h_ref, x_ref, p_ref, wg_ref, wp_ref, gf_ref, y_ref):
    gate = jax.nn.sigmoid(_dot(h_ref[...], wg_ref[...]))
    x3 = x_ref[...] + gate * _dot(p_ref[...].astype(BF16), wp_ref[...])
    y_ref[...] = _rms(x3, gf_ref[...])


def _ple_final(h3, x2, pemb, w_gate, w_ple, g_final):
    n, pd = pemb.shape
    d = x2.shape[1]
    return pl.pallas_call(
        _ple_kernel,
        grid=(n // TM,),
        in_specs=[_row_spec(TM, d), _row_spec(TM, d),
                  _row_spec(TM, pd), _full_spec((d, d)), _full_spec((pd, d)), _full_spec((1, d))],
        out_specs=_row_spec(TM, d),
        out_shape=jax.ShapeDtypeStruct((n, d), F32),
        compiler_params=_params(("parallel",), 48),
        name="ple_final",
    )(h3, x2, pemb, w_gate, w_ple, g_final)


def _rope_tables(pos, half):
    inv = ROPE_THETA ** (-jnp.arange(half, dtype=F32) / half)
    ang = pos[:, None] * inv[None, :]
    cos = jnp.cos(ang)
    sin = jnp.sin(ang)
    pad = jnp.zeros((pos.shape[0], LANES - 2 * half), F32)
    return jnp.concatenate([cos, cos, pad], axis=-1), jnp.concatenate([sin, sin, pad], axis=-1)


def _rot_cols(w, half):
    return jnp.concatenate([-w[..., half:], w[..., :half]], axis=-1)


def _pad_lanes(w):
    return jnp.concatenate([w, jnp.zeros(w.shape[:-1] + (LANES - w.shape[-1],), w.dtype)], axis=-1)


def kernel(x_prompt, x_sample, p_prompt, p_sample, cache_ckv, cache_krope, page_table, state_conv, g_mix_norm, w_in, b_gate, g_q, w_uq, g_kv, w_ukv, w_dw, b_dw, g_conv_ln, b_conv_ln, w_conv_out, w_out, g_ffn_norm, w_ffn_gate, w_ffn_up, w_ffn_down, w_ple, g_ple_norm, w_ple_gate, g_final):
    depth = w_in.shape[0]
    assert depth == 1, "single-layer step only"
    batch, seq, d = x_prompt.shape
    nb, n_new, _ = x_sample.shape
    t_prompt = batch * seq
    t_sample = nb * n_new
    ch = w_dw.shape[2]
    n_state = state_conv.shape[2]
    q_lora = g_q.shape[1]
    kv_lora = g_kv.shape[1]
    n_heads = w_uq.shape[2]
    rope_dim = cache_krope.shape[-1]
    nope_dim = w_uq.shape[3] - rope_dim
    half = rope_dim // 2
    qscale = LOG2E / math.sqrt(nope_dim + rope_dim)
    assert batch == 1 and seq >= n_state
    assert nope_dim == LANES and w_ukv.shape[3] - nope_dim == LANES and 2 * half == rope_dim <= LANES
    assert t_prompt % TM == 0 and t_sample % TM == 0 and t_prompt % TM_CONV == 0 and t_prompt % TQ == 0
    assert nb % SEQ_PER_STEP == 0 and cache_ckv.shape[2] == PAGE_SIZE

    off_glu = 2 * ch
    off_q = off_glu + q_lora
    off_kv = off_q + kv_lora
    off_kr = off_kv + rope_dim

    wi = w_in[0]
    wa = wi[:, :ch].astype(BF16)
    wb = wi[:, ch:off_glu].astype(BF16)
    wq = wi[:, off_glu:off_q].astype(BF16)
    wkv = wi[:, off_q:off_kv].astype(BF16)
    wkr = wi[:, off_kv:off_kr]
    wkr_p = _pad_lanes(wkr).astype(BF16)
    wkt_p = _pad_lanes(_rot_cols(wkr, half)).astype(BF16)
    wgt = wi[:, off_kr:].astype(BF16)
    wq_n = w_uq[0][:, :, :nope_dim].astype(BF16)
    wq_rope = w_uq[0][:, :, nope_dim:]
    wq_r = _pad_lanes(wq_rope).astype(BF16)
    wq_t = _pad_lanes(_rot_cols(wq_rope, half)).astype(BF16)
    flat = lambda w: w.reshape(q_lora, n_heads * LANES)
    flat_t = lambda w: jnp.transpose(w, (1, 2, 0)).reshape(n_heads * LANES, q_lora)
    wuk = w_ukv[0][:, :, :nope_dim]
    wuv = w_ukv[0][:, :, nope_dim:]
    wuk_flat = wuk.reshape(kv_lora, n_heads * nope_dim).astype(BF16)
    wuv_flat_t = jnp.transpose(wuv, (1, 2, 0)).reshape(n_heads * LANES, kv_lora).astype(BF16)
    wuk_t = jnp.transpose(wuk, (1, 2, 0)).astype(BF16)
    wuv_h = jnp.transpose(wuv, (1, 0, 2)).astype(BF16)
    row = lambda a: a.reshape(1, -1)

    n_past = page_table.shape[1] * PAGE_SIZE
    cs_p, sn_p = _rope_tables(jnp.arange(seq, dtype=F32), half)
    cs_s, sn_s = _rope_tables(jnp.tile(n_past + jnp.arange(n_new, dtype=F32), nb), half)

    xp = x_prompt.reshape(t_prompt, d)
    xs = x_sample.reshape(t_sample, d)

    g_mix = row(g_mix_norm[0])
    hp = _norm(xp, g_mix)
    hs = _norm(xs, g_mix)
    up = _glu(hp, wa, wb)
    us = _glu(hs, wa, wb)
    gq = row(g_q[0])
    q_t = _q_cols(hp, wq, gq, flat_t(wq_n), flat_t(wq_r), flat_t(wq_t), (cs_p * qscale).T, (sn_p * qscale).T,
                  n_heads, qscale)
    qcat_s = _q_rows(hs, wq, gq, flat(wq_n), flat(wq_r), flat(wq_t), cs_s * qscale, sn_s * qscale, n_heads, qscale)
    gkv = row(g_kv[0])
    ckv_p, kr_p, kcat, v_t = _kv_proj(hp, wkv, gkv, wkr_p, wkt_p, cs_p, sn_p, n_heads, rope_dim, wuk_flat, wuv_flat_t)
    ckv_s, kr_s = _kv_proj(hs, wkv, gkv, wkr_p, wkt_p, cs_s, sn_s, n_heads, rope_dim)
    bg = row(b_gate[0])
    gates_p = _gates(hp, wgt, bg)
    gates_s = _gates(hs, wgt, bg)

    conv_args = (w_dw[0], row(b_dw[0]), row(g_conv_ln[0]), row(b_conv_ln[0]))
    zp = _conv_prompt(up, *conv_args)
    state_t = jnp.transpose(state_conv[0], (1, 0, 2))
    u_t = jnp.transpose(us.reshape(nb, n_new, ch), (1, 0, 2))
    zs = jnp.transpose(_conv_sample(state_t, u_t, *conv_args), (1, 0, 2)).reshape(t_sample, ch)

    ya_p = _flash(q_t, kcat, v_t, n_heads)
    qabs = _absorb(qcat_s, wuk_t)
    olat = _sample_attn(page_table, qabs, qcat_s, ckv_s, kr_s, cache_ckv, jnp.swapaxes(cache_krope, 2, 3), n_heads)
    ya_s = _uv(olat.reshape(t_sample, n_heads * kv_lora), wuv_h)

    out_w = (w_conv_out[0].astype(BF16), w_out[0].astype(BF16), row(g_ffn_norm[0]))
    ffn_w = (_ffn_tiles(w_ffn_gate[0].astype(BF16)), _ffn_tiles(w_ffn_up[0].astype(BF16)),
             w_ffn_down[0].astype(BF16), row(g_ple_norm[0]))
    ple_w = (w_ple_gate[0].astype(BF16), w_ple[0].astype(BF16), row(g_final))
    outs = []
    for x, z, ya, gates, pemb in ((xp, zp, ya_p, gates_p, p_prompt[0].reshape(t_prompt, -1)),
                                  (xs, zs, ya_s, gates_s, p_sample[0].reshape(t_sample, -1))):
        x1, h2 = _out_proj(x, z, ya, gates, *out_w)
        x2, h3 = _ffn(h2, x1, *ffn_w)
        outs.append(_ple_final(h3, x2, pemb, *ple_w))
    y_prompt, y_sample = outs

    conv_p = up[t_prompt - n_state:].reshape(1, batch, n_state, ch)
    conv_s = jnp.transpose(jnp.concatenate([state_t[n_new:], u_t], axis=0), (1, 0, 2))[None]
    return (y_prompt.reshape(batch, seq, d), y_sample.reshape(nb, n_new, d),
            ckv_p.reshape(1, batch, seq, kv_lora), kr_p.reshape(1, batch, seq, rope_dim), conv_p,
            ckv_s.reshape(1, nb, n_new, kv_lora), kr_s.reshape(1, nb, n_new, rope_dim), conv_s)
```

```python
import functools
import math

import jax
import jax.numpy as jnp
from jax import lax
from jax.experimental import pallas as pl
from jax.experimental.pallas import tpu as pltpu

F32 = jnp.float32
BF16 = jnp.bfloat16

EPS = 1e-6
ROPE_THETA = 10000.0
PAGE_SIZE = 128
LANES = 128
BF16_ROWS = 16
MIB = 1024 * 1024
LOG2E = math.log2(math.e)

TM = 512
TM_OUT = 256
TM_CONV = 256
CONV_ROWS = 64
CONV_HALO = 32
SEQ_PER_STEP = 32
TQ = 1024
TK = 1024
FLASH_HEADS = 2
PAGES_PER_CHUNK = 16
TF = 512


def _params(semantics, vmem_mib):
    return pltpu.CompilerParams(dimension_semantics=semantics, vmem_limit_bytes=vmem_mib * MIB)


def _dot(a, b):
    return jnp.dot(a, b, preferred_element_type=F32)


def _dot_t(a, b):
    return lax.dot_general(a, b, (((1,), (1,)), ((), ())), preferred_element_type=F32)


def _rms(x, g):
    return x * lax.rsqrt(jnp.mean(x * x, axis=-1, keepdims=True) + EPS) * g


def _row_spec(tm, n):
    return pl.BlockSpec((tm, n), lambda i: (i, 0))


def _col_spec(n, tm):
    return pl.BlockSpec((n, tm), lambda i: (0, i))


def _full_spec(shape):
    return pl.BlockSpec(shape, lambda *_: (0,) * len(shape))


def _norm_kernel(x_ref, g_ref, o_ref):
    o_ref[...] = _rms(x_ref[...], g_ref[...]).astype(o_ref.dtype)


def _norm(x, g):
    t, d = x.shape
    return pl.pallas_call(
        _norm_kernel,
        grid=(t // TM,),
        in_specs=[_row_spec(TM, d), _full_spec((1, d))],
        out_specs=_row_spec(TM, d),
        out_shape=jax.ShapeDtypeStruct((t, d), BF16),
        compiler_params=_params(("parallel",), 32),
        name="mix_norm",
    )(x, g)


def _glu_kernel(h_ref, wa_ref, wb_ref, o_ref):
    h = h_ref[...]
    o_ref[...] = _dot(h, wa_ref[...]) * jax.nn.sigmoid(_dot(h, wb_ref[...]))


def _glu(h, wa, wb):
    t, d = h.shape
    c = wa.shape[1]
    return pl.pallas_call(
        _glu_kernel,
        grid=(t // TM,),
        in_specs=[_row_spec(TM, d), _full_spec((d, c)), _full_spec((d, c))],
        out_specs=_row_spec(TM, c),
        out_shape=jax.ShapeDtypeStruct((t, c), F32),
        compiler_params=_params(("parallel",), 40),
        name="glu",
    )(h, wa, wb)


def _q_rows_kernel(h_ref, wq_ref, gq_ref, wn_ref, wr_ref, wt_ref, cs_ref, sn_ref, o_ref, *, n_heads, qscale):
    ql = _rms(_dot(h_ref[...], wq_ref[...]), gq_ref[...]).astype(BF16)
    qn = _dot(ql, wn_ref[...]) * qscale
    qr = _dot(ql, wr_ref[...])
    qt = _dot(ql, wt_ref[...])
    cs = cs_ref[...]
    sn = sn_ref[...]
    for h in range(n_heads):
        lo = h * LANES
        o_ref[:, 2 * lo:2 * lo + LANES] = qn[:, lo:lo + LANES].astype(o_ref.dtype)
        rot = qr[:, lo:lo + LANES] * cs + qt[:, lo:lo + LANES] * sn
        o_ref[:, 2 * lo + LANES:2 * lo + 2 * LANES] = rot.astype(o_ref.dtype)


def _q_rows(h, wq, gq, wn, wr, wt, cs, sn, n_heads, qscale):
    n_rows, d = h.shape
    ql = wq.shape[1]
    hw = wn.shape[1]
    return pl.pallas_call(
        functools.partial(_q_rows_kernel, n_heads=n_heads, qscale=qscale),
        grid=(n_rows // TM,),
        in_specs=[_row_spec(TM, d), _full_spec((d, ql)), _full_spec((1, ql)), _full_spec((ql, hw)),
                  _full_spec((ql, hw)), _full_spec((ql, hw)), _row_spec(TM, LANES), _row_spec(TM, LANES)],
        out_specs=_row_spec(TM, 2 * hw),
        out_shape=jax.ShapeDtypeStruct((n_rows, 2 * hw), BF16),
        compiler_params=_params(("parallel",), 56),
        name="q_rows",
    )(h, wq, gq, wn, wr, wt, cs, sn)


def _q_cols_kernel(h_ref, wq_ref, gq_ref, wn_ref, wr_ref, wt_ref, cs_ref, sn_ref, o_ref, *, n_heads, qscale):
    ql = _rms(_dot(h_ref[...], wq_ref[...]), gq_ref[...])
    ql_t = ql.T.astype(BF16)
    qn = _dot(wn_ref[...], ql_t) * qscale
    qr = _dot(wr_ref[...], ql_t)
    qt = _dot(wt_ref[...], ql_t)
    cs = cs_ref[...]
    sn = sn_ref[...]
    for h in range(n_heads):
        lo = h * LANES
        o_ref[2 * lo:2 * lo + LANES, :] = qn[lo:lo + LANES, :].astype(o_ref.dtype)
        rot = qr[lo:lo + LANES, :] * cs + qt[lo:lo + LANES, :] * sn
        o_ref[2 * lo + LANES:2 * lo + 2 * LANES, :] = rot.astype(o_ref.dtype)


def _q_cols(h, wq, gq, wn_t, wr_t, wt_t, cs_t, sn_t, n_heads, qscale):
    n_rows, d = h.shape
    ql = wq.shape[1]
    hw = wn_t.shape[0]
    return pl.pallas_call(
        functools.partial(_q_cols_kernel, n_heads=n_heads, qscale=qscale),
        grid=(n_rows // TM,),
        in_specs=[_row_spec(TM, d), _full_spec((d, ql)), _full_spec((1, ql)), _full_spec((hw, ql)),
                  _full_spec((hw, ql)), _full_spec((hw, ql)), _col_spec(LANES, TM), _col_spec(LANES, TM)],
        out_specs=_col_spec(2 * hw, TM),
        out_shape=jax.ShapeDtypeStruct((2 * hw, n_rows), BF16),
        compiler_params=_params(("parallel",), 56),
        name="q_cols",
    )(h, wq, gq, wn_t, wr_t, wt_t, cs_t, sn_t)


def _kv_kernel(h_ref, wkv_ref, gkv_ref, wkr_ref, wkt_ref, cs_ref, sn_ref, *rest, n_heads, rope_dim, v_rows, prompt):
    if prompt:
        wuk_ref, wuv_ref, ckv_ref, kr_ref, kcat_ref, vt_ref = rest
    else:
        ckv_ref, kr_ref = rest
    h = h_ref[...]
    ckv = _rms(_dot(h, wkv_ref[...]), gkv_ref[...])
    ckv_ref[...] = ckv
    kr = _dot(h, wkr_ref[...]) * cs_ref[...] + _dot(h, wkt_ref[...]) * sn_ref[...]
    kr_ref[...] = kr[:, :rope_dim]
    if not prompt:
        return
    kr_b = kr.astype(kcat_ref.dtype)
    kn = _dot(ckv.astype(BF16), wuk_ref[...]).astype(kcat_ref.dtype)
    vt = _dot(wuv_ref[...], ckv.T.astype(BF16)).astype(vt_ref.dtype)
    tm = h.shape[0]
    ones_rows = jnp.where(lax.broadcasted_iota(jnp.int32, (v_rows - LANES, tm), 0) == 0, 1.0, 0.0).astype(vt_ref.dtype)
    for hd in range(n_heads):
        lo = hd * LANES
        kcat_ref[:, 2 * lo:2 * lo + LANES] = kn[:, lo:lo + LANES]
        kcat_ref[:, 2 * lo + LANES:2 * lo + 2 * LANES] = kr_b
        vt_ref[hd * v_rows:hd * v_rows + LANES, :] = vt[lo:lo + LANES, :]
        vt_ref[hd * v_rows + LANES:(hd + 1) * v_rows, :] = ones_rows


def _kv_proj(h, wkv, gkv, wkr, wkt, cs, sn, n_heads, rope_dim, wuk=None, wuv_t=None):
    n_rows, d = h.shape
    kl = wkv.shape[1]
    prompt = wuk is not None
    v_rows = LANES + BF16_ROWS
    in_specs = [_row_spec(TM, d), _full_spec((d, kl)), _full_spec((1, kl)), _full_spec((d, LANES)),
                _full_spec((d, LANES)), _row_spec(TM, LANES), _row_spec(TM, LANES)]
    out_specs = [_row_spec(TM, kl), _row_spec(TM, rope_dim)]
    out_shape = [jax.ShapeDtypeStruct((n_rows, kl), F32), jax.ShapeDtypeStruct((n_rows, rope_dim), F32)]
    args = [h, wkv, gkv, wkr, wkt, cs, sn]
    if prompt:
        hw = wuk.shape[1]
        in_specs += [_full_spec((kl, hw)), _full_spec((hw, kl))]
        out_specs += [_row_spec(TM, 2 * hw), _col_spec(n_heads * v_rows, TM)]
        out_shape += [jax.ShapeDtypeStruct((n_rows, 2 * hw), BF16),
                      jax.ShapeDtypeStruct((n_heads * v_rows, n_rows), BF16)]
        args += [wuk, wuv_t]
    return pl.pallas_call(
        functools.partial(_kv_kernel, n_heads=n_heads, rope_dim=rope_dim, v_rows=v_rows, prompt=prompt),
        grid=(n_rows // TM,),
        in_specs=in_specs,
        out_specs=out_specs,
        out_shape=out_shape,
        compiler_params=_params(("parallel",), 56),
        name="kv_proj_prompt" if prompt else "kv_proj_sample",
    )(*args)


def _gate_kernel(h_ref, w_ref, b_ref, o_ref):
    o_ref[...] = jax.nn.sigmoid(_dot(h_ref[...], w_ref[...]) + b_ref[...]).astype(o_ref.dtype)


def _gates(h, w, b, tn=1024):
    t, d = h.shape
    n = w.shape[1]
    return pl.pallas_call(
        _gate_kernel,
        grid=(n // tn, t // TM),
        in_specs=[pl.BlockSpec((TM, d), lambda j, i: (i, 0)), pl.BlockSpec((d, tn), lambda j, i: (0, j)),
                  pl.BlockSpec((1, tn), lambda j, i: (0, j))],
        out_specs=pl.BlockSpec((TM, tn), lambda j, i: (i, j)),
        out_shape=jax.ShapeDtypeStruct((t, n), BF16),
        compiler_params=_params(("parallel", "parallel"), 40),
        name="branch_gates",
    )(h, w, b)


def _ln_silu(y, g, b):
    mu = jnp.mean(y, axis=-1, keepdims=True)
    d = y - mu
    var = jnp.mean(d * d, axis=-1, keepdims=True)
    z = d * lax.rsqrt(var + EPS) * g + b
    return z * jax.nn.sigmoid(z)


def _conv_prompt_kernel(prev_ref, cur_ref, w_ref, bdw_ref, g_ref, b_ref, o_ref, ext_ref, y_ref, *, width):
    i = pl.program_id(0)
    tm, ch = cur_ref.shape

    @pl.when(i == 0)
    def _():
        ext_ref[0:CONV_HALO, :] = jnp.zeros((CONV_HALO, ch), F32)

    @pl.when(i > 0)
    def _():
        ext_ref[0:CONV_HALO, :] = prev_ref[...]

    ext_ref[CONV_HALO:CONV_HALO + tm, :] = cur_ref[...]

    lead = CONV_HALO - (width - 1)
    by_shift = {}
    for k in range(width):
        a, s = divmod(lead + k, 8)
        by_shift.setdefault(s, []).append((a, k))
    for r0 in range(0, tm, CONV_ROWS):
        for c0 in range(0, ch, LANES):
            acc = jnp.zeros((CONV_ROWS, LANES), F32) + bdw_ref[:, c0:c0 + LANES]
            for s, taps in by_shift.items():
                a_lo = min(a for a, _ in taps)
                a_hi = max(a for a, _ in taps)
                win = ext_ref[r0 + s + 8 * a_lo:r0 + s + 8 * a_hi + CONV_ROWS, c0:c0 + LANES]
                for a, k in taps:
                    off = 8 * (a - a_lo)
                    acc = acc + win[off:off + CONV_ROWS, :] * w_ref[k:k + 1, c0:c0 + LANES]
            y_ref[r0:r0 + CONV_ROWS, c0:c0 + LANES] = acc
    o_ref[...] = _ln_silu(y_ref[...], g_ref[...], b_ref[...]).astype(o_ref.dtype)


def _conv_prompt(u, w_dw, b_dw, g_ln, b_ln):
    t, ch = u.shape
    width = w_dw.shape[0]
    per = TM_CONV // CONV_HALO
    return pl.pallas_call(
        functools.partial(_conv_prompt_kernel, width=width),
        grid=(t // TM_CONV,),
        in_specs=[pl.BlockSpec((CONV_HALO, ch), lambda i: (jnp.maximum(i * per - 1, 0), 0)),
                  _row_spec(TM_CONV, ch), _full_spec((width, ch)), _full_spec((1, ch)),
                  _full_spec((1, ch)), _full_spec((1, ch))],
        out_specs=_row_spec(TM_CONV, ch),
        out_shape=jax.ShapeDtypeStruct((t, ch), BF16),
        scratch_shapes=[pltpu.VMEM((TM_CONV + CONV_HALO, ch), F32), pltpu.VMEM((TM_CONV, ch), F32)],
        compiler_params=_params(("arbitrary",), 32),
        name="conv_prompt",
    )(u, u, w_dw, b_dw, g_ln, b_ln)


def _conv_sample_kernel(st_ref, u_ref, w_ref, bdw_ref, g_ref, b_ref, o_ref):
    n_state = st_ref.shape[0]
    n_new = u_ref.shape[0]
    width = w_ref.shape[0]
    for t in range(n_new):
        acc = jnp.zeros(o_ref.shape[1:], F32) + bdw_ref[...]
        for k in range(width):
            j = t + k
            src = st_ref[j] if j < n_state else u_ref[j - n_state]
            acc = acc + src * w_ref[k:k + 1, :]
        o_ref[t] = _ln_silu(acc, g_ref[...], b_ref[...]).astype(o_ref.dtype)


def _conv_sample(state_t, u_t, w_dw, b_dw, g_ln, b_ln):
    n_state, nb, ch = state_t.shape
    n_new = u_t.shape[0]
    width = w_dw.shape[0]
    assert width == n_state + 1
    return pl.pallas_call(
        _conv_sample_kernel,
        grid=(nb // SEQ_PER_STEP,),
        in_specs=[pl.BlockSpec((n_state, SEQ_PER_STEP, ch), lambda i: (0, i, 0)),
                  pl.BlockSpec((n_new, SEQ_PER_STEP, ch), lambda i: (0, i, 0)),
                  _full_spec((width, ch)), _full_spec((1, ch)), _full_spec((1, ch)), _full_spec((1, ch))],
        out_specs=pl.BlockSpec((n_new, SEQ_PER_STEP, ch), lambda i: (0, i, 0)),
        out_shape=jax.ShapeDtypeStruct((n_new, nb, ch), BF16),
        compiler_params=_params(("parallel",), 32),
        name="conv_sample",
    )(state_t, u_t, w_dw, b_dw, g_ln, b_ln)


def _flash_kernel(qt_ref, k_ref, vt_ref, o_ref):
    qi = pl.program_id(1)
    tq = qt_ref.shape[1]
    dq = qt_ref.shape[0] // FLASH_HEADS
    v_rows = vt_ref.shape[0] // FLASH_HEADS
    dv = o_ref.shape[1] // FLASH_HEADS

    def block(off, carry, first_key):
        out = []
        for hd in range(FLASH_HEADS):
            m, acc = carry[hd]
            s = _dot(k_ref[pl.ds(off, TK), hd * dq:(hd + 1) * dq], qt_ref[hd * dq:(hd + 1) * dq, :])
            if first_key is not None:
                key = first_key + lax.broadcasted_iota(jnp.int32, (TK, tq), 0)
                qry = lax.broadcasted_iota(jnp.int32, (TK, tq), 1)
                s = jnp.where(key <= qry, s, -jnp.inf)
            m_new = jnp.maximum(m, jnp.max(s, axis=0, keepdims=True))
            p = jnp.exp2(s - m_new).astype(vt_ref.dtype)
            pv = _dot(vt_ref[hd * v_rows:(hd + 1) * v_rows, pl.ds(off, TK)], p)
            out.append((m_new, jnp.exp2(m - m_new) * acc + pv))
        return tuple(out)

    per = tq // TK
    init = tuple((jnp.full((1, tq), -jnp.inf, F32), jnp.zeros((v_rows, tq), F32)) for _ in range(FLASH_HEADS))
    carry = lax.fori_loop(0, qi * per, lambda kb, c: block(pl.multiple_of(kb * TK, TK), c, None), init)
    for j in range(per):
        carry = block(pl.multiple_of(qi * tq + j * TK, TK), carry, j * TK)
    for hd in range(FLASH_HEADS):
        acc = carry[hd][1]
        o_ref[:, hd * dv:(hd + 1) * dv] = (acc[:dv, :] / acc[dv:dv + 1, :]).T.astype(o_ref.dtype)


def _flash(q_t, kcat, v_t, n_heads):
    t_prompt = q_t.shape[1]
    dq = q_t.shape[0] // n_heads
    v_rows = v_t.shape[0] // n_heads
    nh = FLASH_HEADS
    assert n_heads % nh == 0
    return pl.pallas_call(
        _flash_kernel,
        grid=(n_heads // nh, t_prompt // TQ),
        in_specs=[pl.BlockSpec((nh * dq, TQ), lambda h, i: (h, i)),
                  pl.BlockSpec((t_prompt, nh * dq), lambda h, i: (0, h)),
                  pl.BlockSpec((nh * v_rows, t_prompt), lambda h, i: (h, 0))],
        out_specs=pl.BlockSpec((TQ, nh * LANES), lambda h, i: (i, h)),
        out_shape=jax.ShapeDtypeStruct((t_prompt, n_heads * LANES), BF16),
        compiler_params=_params(("parallel", "arbitrary"), 56),
        name="prompt_attention",
    )(q_t, kcat, v_t)


def _head_dot_kernel(q_ref, w_ref, o_ref):
    o_ref[...] = _dot(q_ref[...], w_ref[...]).astype(o_ref.dtype)


def _absorb(qcat_s, wuk_t):
    n_heads, dn, kl = wuk_t.shape
    ts = qcat_s.shape[0]
    return pl.pallas_call(
        _head_dot_kernel,
        grid=(n_heads,),
        in_specs=[pl.BlockSpec((ts, dn), lambda h: (0, 2 * h)),
                  pl.BlockSpec((None, dn, kl), lambda h: (h, 0, 0))],
        out_specs=pl.BlockSpec((ts, kl), lambda h: (0, h)),
        out_shape=jax.ShapeDtypeStruct((ts, n_heads * kl), BF16),
        compiler_params=_params(("parallel",), 32),
        name="absorb_q",
    )(qcat_s, wuk_t)


def _softmax_step(s, v, m, l, acc):
    m_new = jnp.maximum(m, jnp.max(s, axis=-1, keepdims=True))
    alpha = jnp.exp2(m - m_new)
    p = jnp.exp2(s - m_new)
    l_new = alpha * l + jnp.sum(p, axis=-1, keepdims=True)
    acc_new = alpha * acc + _dot(p.astype(v.dtype), v)
    return m_new, l_new, acc_new


def _sample_attn_kernel(pt_ref, qa_ref, qr_ref, cn_ref, kn_ref, poolc_ref, poolr_ref, o_ref,
                        kc_buf, kr_buf, kc_s, kr_s, sem_c, sem_r, *, n_heads, rope_dim, n_pages):
    b = pl.program_id(0)
    n_chunks = n_pages // PAGES_PER_CHUNK
    total = pl.num_programs(0) * n_chunks
    qa = qa_ref[...]
    qr = qr_ref[:, :rope_dim]
    rows = qa.shape[0]

    def chunk_copies(g, slot):
        base = lax.rem(g, total) * PAGES_PER_CHUNK
        out = []
        for i in range(PAGES_PER_CHUNK):
            page = pt_ref[base + i]
            out.append(pltpu.make_async_copy(poolc_ref.at[0, page],
                                             kc_buf.at[slot, pl.ds(i * PAGE_SIZE, PAGE_SIZE)], sem_c.at[slot]))
            out.append(pltpu.make_async_copy(poolr_ref.at[0, page], kr_buf.at[slot, i], sem_r.at[slot]))
        return out

    def start_chunk(g, slot):
        for n, cp in enumerate(chunk_copies(g, slot)):
            cp.start(priority=(n // 2 + n) % 2)

    def wait_chunk(g, slot):
        for cp in chunk_copies(g, slot):
            cp.wait()

    def cast_chunk(slot):
        kc_s[slot] = kc_buf[slot].astype(kc_s.dtype)
        for i in range(PAGES_PER_CHUNK):
            kr_s[slot, :, i * PAGE_SIZE:(i + 1) * PAGE_SIZE] = kr_buf[slot, i].astype(kr_s.dtype)

    @pl.when(b == 0)
    def _():
        start_chunk(0, 0)
        start_chunk(1, 1)
        wait_chunk(0, 0)
        cast_chunk(0)
        start_chunk(2, 0)

    qa32 = qa.astype(F32)
    qr32 = qr.astype(F32)
    n_new = cn_ref.shape[0]
    tok = lax.broadcasted_iota(jnp.int32, (rows, 1), 0) // n_heads
    scores = []
    for j in range(n_new):
        sj = (jnp.sum(qa32 * cn_ref[j:j + 1, :], axis=-1, keepdims=True)
              + jnp.sum(qr32 * kn_ref[j:j + 1, :], axis=-1, keepdims=True))
        scores.append(jnp.where(tok >= j, sj, -jnp.inf))
    m = scores[0]
    for sj in scores[1:]:
        m = jnp.maximum(m, sj)
    l = jnp.zeros_like(m)
    acc = jnp.zeros((rows, kc_s.shape[-1]), F32)
    for j, sj in enumerate(scores):
        pj = jnp.exp2(sj - m)
        l = l + pj
        acc = acc + pj * cn_ref[j:j + 1, :]

    def pair(cc, carry):
        for slot in range(2):
            g = b * n_chunks + 2 * cc + slot
            nxt = 1 - slot
            wait_chunk(g + 1, nxt)
            cast_chunk(nxt)
            start_chunk(g + 3, nxt)
            kc = kc_s[slot]
            s = _dot_t(qa, kc) + _dot(qr, kr_s[slot])
            carry = _softmax_step(s, kc, *carry)
        return carry

    m, l, acc = lax.fori_loop(0, n_chunks // 2, pair, (m, l, acc))
    o_ref[...] = (acc / l).astype(o_ref.dtype)

    @pl.when(b == pl.num_programs(0) - 1)
    def _():
        wait_chunk(total + 1, 1)
        wait_chunk(total + 2, 0)


def _sample_attn(page_table, qabs, qcat_s, ckv_s, kr_s, pool_c, pool_r_t, n_heads):
    nb, n_pages = page_table.shape
    kl = pool_c.shape[-1]
    rope_dim = pool_r_t.shape[-2]
    ts = qabs.shape[0]
    n_new = ts // nb
    rows = n_new * n_heads
    npc = PAGES_PER_CHUNK
    dq = qcat_s.shape[1] // n_heads
    qa3 = qabs.reshape(nb, rows, kl)
    qc3 = qcat_s.reshape(nb, rows, dq)
    cn3 = ckv_s.reshape(nb, n_new, kl)
    kn3 = kr_s.reshape(nb, n_new, rope_dim)
    pt = page_table.reshape(-1)
    assert n_pages % (2 * npc) == 0

    grid_spec = pltpu.PrefetchScalarGridSpec(
        num_scalar_prefetch=1,
        grid=(nb,),
        in_specs=[pl.BlockSpec((None, rows, kl), lambda b, pt_ref: (b, 0, 0)),
                  pl.BlockSpec((None, rows, LANES), lambda b, pt_ref: (b, 0, 1)),
                  pl.BlockSpec((None, n_new, kl), lambda b, pt_ref: (b, 0, 0)),
                  pl.BlockSpec((None, n_new, rope_dim), lambda b, pt_ref: (b, 0, 0)),
                  pl.BlockSpec(memory_space=pl.ANY), pl.BlockSpec(memory_space=pl.ANY)],
        out_specs=pl.BlockSpec((None, rows, kl), lambda b, pt_ref: (b, 0, 0)),
        scratch_shapes=[pltpu.VMEM((2, npc * PAGE_SIZE, kl), F32), pltpu.VMEM((2, npc, rope_dim, PAGE_SIZE), F32),
                        pltpu.VMEM((2, npc * PAGE_SIZE, kl), BF16), pltpu.VMEM((2, rope_dim, npc * PAGE_SIZE), BF16),
                        pltpu.SemaphoreType.DMA((2,)), pltpu.SemaphoreType.DMA((2,))],
    )
    return pl.pallas_call(
        functools.partial(_sample_attn_kernel, n_heads=n_heads, rope_dim=rope_dim, n_pages=n_pages),
        grid_spec=grid_spec,
        out_shape=jax.ShapeDtypeStruct((nb, rows, kl), BF16),
        compiler_params=_params(("arbitrary",), 40),
        name="sample_attention",
    )(pt, qa3, qc3, cn3, kn3, pool_c, pool_r_t)


def _uv(olat, wuv_h):
    n_heads, kl, dv = wuv_h.shape
    ts = olat.shape[0]
    return pl.pallas_call(
        _head_dot_kernel,
        grid=(n_heads,),
        in_specs=[pl.BlockSpec((ts, kl), lambda h: (0, h)),
                  pl.BlockSpec((None, kl, dv), lambda h: (h, 0, 0))],
        out_specs=pl.BlockSpec((ts, dv), lambda h: (0, h)),
        out_shape=jax.ShapeDtypeStruct((ts, n_heads * dv), BF16),
        compiler_params=_params(("parallel",), 32),
        name="value_up",
    )(olat, wuv_h)


def _out_kernel(x_ref, z_ref, ya_ref, g_ref, wc_ref, wo_ref, gn_ref, xo_ref, ho_ref):
    d = x_ref.shape[1]
    yc = _dot(z_ref[...], wc_ref[...])
    mix = g_ref[:, :d].astype(F32) * yc + g_ref[:, d:].astype(F32) * ya_ref[...].astype(F32)
    x1 = x_ref[...] + _dot(mix.astype(BF16), wo_ref[...])
    xo_ref[...] = x1
    ho_ref[...] = _rms(x1, gn_ref[...]).astype(ho_ref.dtype)


def _out_proj(x, z, y_attn, gates, w_conv_out, w_out, g_ffn):
    t, d = x.shape
    ch = z.shape[1]
    return pl.pallas_call(
        _out_kernel,
        grid=(t // TM_OUT,),
        in_specs=[_row_spec(TM_OUT, d), _row_spec(TM_OUT, ch), _row_spec(TM_OUT, d), _row_spec(TM_OUT, 2 * d),
                  _full_spec((ch, d)), _full_spec((d, d)), _full_spec((1, d))],
        out_specs=[_row_spec(TM_OUT, d), _row_spec(TM_OUT, d)],
        out_shape=[jax.ShapeDtypeStruct((t, d), F32), jax.ShapeDtypeStruct((t, d), BF16)],
        compiler_params=_params(("parallel",), 56),
        name="out_proj",
    )(x, z, y_attn, gates, w_conv_out, w_out, g_ffn)


def _ffn_kernel(h_ref, x_ref, wg_ref, wu_ref, wd_ref, gn_ref, xo_ref, ho_ref, acc_ref):
    f = pl.program_id(1)

    @pl.when(f == 0)
    def _():
        acc_ref[...] = x_ref[...]

    h = h_ref[...]
    g = _dot(h, wg_ref[...])
    a = (g * jax.nn.sigmoid(g) * _dot(h, wu_ref[...])).astype(BF16)
    acc_ref[...] += _dot(a, wd_ref[...])

    @pl.when(f == pl.num_programs(1) - 1)
    def _():
        x2 = acc_ref[...]
        xo_ref[...] = x2
        ho_ref[...] = _rms(x2, gn_ref[...]).astype(ho_ref.dtype)


def _ffn(h2, x1, wg, wu, wd, g_ple):
    t, d = x1.shape
    dff = wg.shape[1]
    return pl.pallas_call(
        _ffn_kernel,
        grid=(t // TM, dff // TF),
        in_specs=[pl.BlockSpec((TM, d), lambda i, f: (i, 0)), pl.BlockSpec((TM, d), lambda i, f: (i, 0)),
                  pl.BlockSpec((d, TF), lambda i, f: (0, f)), pl.BlockSpec((d, TF), lambda i, f: (0, f)),
                  pl.BlockSpec((TF, d), lambda i, f: (f, 0)), pl.BlockSpec((1, d), lambda i, f: (0, 0))],
        out_specs=[pl.BlockSpec((TM, d), lambda i, f: (i, 0)), pl.BlockSpec((TM, d), lambda i, f: (i, 0))],
        out_shape=[jax.ShapeDtypeStruct((t, d), F32), jax.ShapeDtypeStruct((t, d), BF16)],
        scratch_shapes=[pltpu.VMEM((TM, d), F32)],
        compiler_params=_params(("parallel", "arbitrary"), 56),
        name="swiglu",
    )(h2, x1, wg, wu, wd, g_ple)


def _ple_kernel(h_ref, x_ref, p_ref, wg_ref, wp_ref, gf_ref, y_ref):
    gate = jax.nn.sigmoid(_dot(h_ref[...], wg_ref[...]))
    x3 = x_ref[...] + gate * _dot(p_ref[...].astype(BF16), wp_ref[...])
    y_ref[...] = _rms(x3, gf_ref[...])


def _ple_final(h3, x2, pemb, w_gate, w_ple, g_final):
    n, pd = pemb.shape
    d = x2.shape[1]
    return pl.pallas_call(
        _ple_kernel,
        grid=(n // TM,),
        in_specs=[_row_spec(TM, d), _row_spec(TM, d),
                  _row_spec(TM, pd), _full_spec((d, d)), _full_spec((pd, d)), _full_spec((1, d))],
        out_specs=_row_spec(TM, d),
        out_shape=jax.ShapeDtypeStruct((n, d), F32),
        compiler_params=_params(("parallel",), 48),
        name="ple_final",
    )(h3, x2, pemb, w_gate, w_ple, g_final)


def _rope_tables(pos, half):
    inv = ROPE_THETA ** (-jnp.arange(half, dtype=F32) / half)
    ang = pos[:, None] * inv[None, :]
    cos = jnp.cos(ang)
    sin = jnp.sin(ang)
    pad = jnp.zeros((pos.shape[0], LANES - 2 * half), F32)
    return jnp.concatenate([cos, cos, pad], axis=-1), jnp.concatenate([sin, sin, pad], axis=-1)


def _rot_cols(w, half):
    return jnp.concatenate([-w[..., half:], w[..., :half]], axis=-1)


def _pad_lanes(w):
    return jnp.concatenate([w, jnp.zeros(w.shape[:-1] + (LANES - w.shape[-1],), w.dtype)], axis=-1)


def kernel(x_prompt, x_sample, p_prompt, p_sample, cache_ckv, cache_krope, page_table, state_conv, g_mix_norm, w_in, b_gate, g_q, w_uq, g_kv, w_ukv, w_dw, b_dw, g_conv_ln, b_conv_ln, w_conv_out, w_out, g_ffn_norm, w_ffn_gate, w_ffn_up, w_ffn_down, w_ple, g_ple_norm, w_ple_gate, g_final):
    depth = w_in.shape[0]
    assert depth == 1, "single-layer step only"
    batch, seq, d = x_prompt.shape
    nb, n_new, _ = x_sample.shape
    t_prompt = batch * seq
    t_sample = nb * n_new
    ch = w_dw.shape[2]
    n_state = state_conv.shape[2]
    q_lora = g_q.shape[1]
    kv_lora = g_kv.shape[1]
    n_heads = w_uq.shape[2]
    rope_dim = cache_krope.shape[-1]
    nope_dim = w_uq.shape[3] - rope_dim
    half = rope_dim // 2
    qscale = LOG2E / math.sqrt(nope_dim + rope_dim)
    assert batch == 1 and seq >= n_state
    assert nope_dim == LANES and w_ukv.shape[3] - nope_dim == LANES and 2 * half == rope_dim <= LANES
    assert t_prompt % TM == 0 and t_sample % TM == 0 and t_prompt % TM_CONV == 0 and t_prompt % TQ == 0
    assert nb % SEQ_PER_STEP == 0 and cache_ckv.shape[2] == PAGE_SIZE

    off_glu = 2 * ch
    off_q = off_glu + q_lora
    off_kv = off_q + kv_lora
    off_kr = off_kv + rope_dim

    wi = w_in[0]
    wa = wi[:, :ch].astype(BF16)
    wb = wi[:, ch:off_glu].astype(BF16)
    wq = wi[:, off_glu:off_q].astype(BF16)
    wkv = wi[:, off_q:off_kv].astype(BF16)
    wkr = wi[:, off_kv:off_kr]
    wkr_p = _pad_lanes(wkr).astype(BF16)
    wkt_p = _pad_lanes(_rot_cols(wkr, half)).astype(BF16)
    wgt = wi[:, off_kr:].astype(BF16)
    wq_n = w_uq[0][:, :, :nope_dim].astype(BF16)
    wq_rope = w_uq[0][:, :, nope_dim:]
    wq_r = _pad_lanes(wq_rope).astype(BF16)
    wq_t = _pad_lanes(_rot_cols(wq_rope, half)).astype(BF16)
    flat = lambda w: w.reshape(q_lora, n_heads * LANES)
    flat_t = lambda w: jnp.transpose(w, (1, 2, 0)).reshape(n_heads * LANES, q_lora)
    wuk = w_ukv[0][:, :, :nope_dim]
    wuv = w_ukv[0][:, :, nope_dim:]
    wuk_flat = wuk.reshape(kv_lora, n_heads * nope_dim).astype(BF16)
    wuv_flat_t = jnp.transpose(wuv, (1, 2, 0)).reshape(n_heads * LANES, kv_lora).astype(BF16)
    wuk_t = jnp.transpose(wuk, (1, 2, 0)).astype(BF16)
    wuv_h = jnp.transpose(wuv, (1, 0, 2)).astype(BF16)
    row = lambda a: a.reshape(1, -1)

    n_past = page_table.shape[1] * PAGE_SIZE
    cs_p, sn_p = _rope_tables(jnp.arange(seq, dtype=F32), half)
    cs_s, sn_s = _rope_tables(jnp.tile(n_past + jnp.arange(n_new, dtype=F32), nb), half)

    xp = x_prompt.reshape(t_prompt, d)
    xs = x_sample.reshape(t_sample, d)

    g_mix = row(g_mix_norm[0])
    hp = _norm(xp, g_mix)
    hs = _norm(xs, g_mix)
    up = _glu(hp, wa, wb)
    us = _glu(hs, wa, wb)
    gq = row(g_q[0])
    q_t = _q_cols(hp, wq, gq, flat_t(wq_n), flat_t(wq_r), flat_t(wq_t), (cs_p * qscale).T, (sn_p * qscale).T,
                  n_heads, qscale)
    qcat_s = _q_rows(hs, wq, gq, flat(wq_n), flat(wq_r), flat(wq_t), cs_s * qscale, sn_s * qscale, n_heads, qscale)
    gkv = row(g_kv[0])
    ckv_p, kr_p, kcat, v_t = _kv_proj(hp, wkv, gkv, wkr_p, wkt_p, cs_p, sn_p, n_heads, rope_dim, wuk_flat, wuv_flat_t)
    ckv_s, kr_s = _kv_proj(hs, wkv, gkv, wkr_p, wkt_p, cs_s, sn_s, n_heads, rope_dim)
    bg = row(b_gate[0])
    gates_p = _gates(hp, wgt, bg)
    gates_s = _gates(hs, wgt, bg)

    conv_args = (w_dw[0], row(b_dw[0]), row(g_conv_ln[0]), row(b_conv_ln[0]))
    zp = _conv_prompt(up, *conv_args)
    state_t = jnp.transpose(state_conv[0], (1, 0, 2))
    u_t = jnp.transpose(us.reshape(nb, n_new, ch), (1, 0, 2))
    zs = jnp.transpose(_conv_sample(state_t, u_t, *conv_args), (1, 0, 2)).reshape(t_sample, ch)

    ya_p = _flash(q_t, kcat, v_t, n_heads)
    qabs = _absorb(qcat_s, wuk_t)
    olat = _sample_attn(page_table, qabs, qcat_s, ckv_s, kr_s, cache_ckv, jnp.swapaxes(cache_krope, 2, 3), n_heads)
    ya_s = _uv(olat.reshape(t_sample, n_heads * kv_lora), wuv_h)

    out_w = (w_conv_out[0].astype(BF16), w_out[0].astype(BF16), row(g_ffn_norm[0]))
    ffn_w = (w_ffn_gate[0].astype(BF16), w_ffn_up[0].astype(BF16), w_ffn_down[0].astype(BF16), row(g_ple_norm[0]))
    ple_w = (w_ple_gate[0].astype(BF16), w_ple[0].astype(BF16), row(g_final))
    outs = []
    for x, z, ya, gates, pemb in ((xp, zp, ya_p, gates_p, p_prompt[0].reshape(t_prompt, -1)),
                                  (xs, zs, ya_s, gates_s, p_sample[0].reshape(t_sample, -1))):
        x1, h2 = _out_proj(x, z, ya, gates, *out_w)
        x2, h3 = _ffn(h2, x1, *ffn_w)
        outs.append(_ple_final(h3, x2, pemb, *ple_w))
    y_prompt, y_sample = outs

    conv_p = up[t_prompt - n_state:].reshape(1, batch, n_state, ch)
    conv_s = jnp.transpose(jnp.concatenate([state_t[n_new:], u_t], axis=0), (1, 0, 2))[None]
    return (y_prompt.reshape(batch, seq, d), y_sample.reshape(nb, n_new, d),
            ckv_p.reshape(1, batch, seq, kv_lora), kr_p.reshape(1, batch, seq, rope_dim), conv_p,
            ckv_s.reshape(1, nb, n_new, kv_lora), kr_s.reshape(1, nb, n_new, rope_dim), conv_s)
```
